```python
import jax, jax.numpy as jnp
from jax import lax
import numpy as np

D_MODEL = 1024
BATCH = 8
SEQ = 2048
DEPTH = 4
DEC_BATCH = 32
DEC_SEQ = 8
PAST_LEN = 8192
PAGE_SIZE = 128

N_HEADS = 16
HEAD_DIM = D_MODEL // N_HEADS
N_MIXERS = 3
N_FOX = (DEPTH + 2) // 3
N_RWKV = (DEPTH + 1) // 3
N_MOBA = DEPTH // 3
Q_BLOCK = 128
MOBA_BLOCK = 256
MOBA_TOPK = 3
RWKV_LORA_W = 64
RWKV_LORA_A = 64
RMS_EPS = 1e-6
GN_EPS = 64e-5
FORGET_BIAS = 3.0
N_PAGES = PAST_LEN // PAGE_SIZE
N_POOL = DEC_BATCH * N_PAGES + (DEC_BATCH * N_PAGES) // 4

kernel_name = 'hybrid_fox_rwkv7_moba_step'


def rms_norm(x, g):
    xf = x.astype(jnp.float32)
    y = xf * lax.rsqrt(jnp.mean(xf * xf, axis=-1, keepdims=True) + RMS_EPS)
    return (y * g.astype(jnp.float32)).astype(x.dtype)


def alibi_slopes():
    return jnp.asarray(2.0 ** (-8.0 * np.arange(1, N_HEADS + 1, dtype=np.float32) / N_HEADS), dtype=jnp.float32)


def gather_past(pool, layer, page_table):
    rows = pool[layer, page_table]
    return rows.reshape((page_table.shape[0], page_table.shape[1] * PAGE_SIZE) + pool.shape[3:])


def paged_rows(pool, layer, page_table, pos, head=None):
    bidx = jnp.arange(page_table.shape[0]).reshape((-1,) + (1,) * (pos.ndim - 1))
    phys = page_table[bidx, pos // PAGE_SIZE]
    off = pos % PAGE_SIZE
    if head is None:
        return pool[layer, phys, off]
    return pool[layer, phys, off, head]


def gated_out(o, g, w_out):
    o = o.reshape(g.shape).astype(g.dtype)
    return (o * jax.nn.silu(g)) @ w_out


def attn_project(xn, w_in, q_g, k_g):
    nb, t, _ = xn.shape
    d = D_MODEL
    hd = (nb, t, N_HEADS, HEAD_DIM)
    proj = xn @ w_in
    q = rms_norm(proj[..., :d].reshape(hd), q_g)
    k = rms_norm(proj[..., d:2 * d].reshape(hd), k_g)
    v = proj[..., 2 * d:3 * d].reshape(hd)
    g = proj[..., 3 * d:4 * d]
    return q, k, v, g, proj[..., 4 * d:]


def fox_prompt(q, k, v, logf):
    nb, s = q.shape[:2]
    scale = HEAD_DIM ** -0.5
    c = jnp.cumsum(logf, axis=1).transpose(0, 2, 1)
    kpos = jnp.arange(s)

    def block(j):
        start = j * Q_BLOCK
        q_blk = lax.dynamic_slice_in_dim(q, start, Q_BLOCK, axis=1)
        c_blk = lax.dynamic_slice_in_dim(c, start, Q_BLOCK, axis=2)
        qpos = start + jnp.arange(Q_BLOCK)
        sc = jnp.einsum('bqhd,bkhd->bhqk', q_blk, k).astype(jnp.float32) * scale
        sc = sc + c_blk[..., :, None] - c[..., None, :]
        sc = jnp.where(kpos[None, :] <= qpos[:, None], sc, -jnp.inf)
        p = jax.nn.softmax(sc, axis=-1).astype(v.dtype)
        return jnp.einsum('bhqk,bkhd->bqhd', p, v)

    o = lax.map(block, jnp.arange(s // Q_BLOCK))
    return o.transpose(1, 0, 2, 3, 4).reshape(nb, s, N_HEADS, HEAD_DIM)


def fox_sample(q, k, v, logf, pool_k, pool_v, pool_lf, layer, page_table):
    t = q.shape[1]
    scale = HEAD_DIM ** -0.5
    k_past = gather_past(pool_k, layer, page_table)
    v_past = gather_past(pool_v, layer, page_table)
    c_past = jnp.cumsum(gather_past(pool_lf, layer, page_table).astype(jnp.float32), axis=1)
    c_new = c_past[:, -1:] + jnp.cumsum(logf, axis=1)
    c_past = c_past.transpose(0, 2, 1)
    c_new = c_new.transpose(0, 2, 1)
    s_past = jnp.einsum('bqhd,bkhd->bhqk', q, k_past).astype(jnp.float32) * scale + c_new[..., :, None] - c_past[..., None, :]
    s_new = jnp.einsum('bqhd,bkhd->bhqk', q, k).astype(jnp.float32) * scale + c_new[..., :, None] - c_new[..., None, :]
    s_new = jnp.where(np.tril(np.ones((t, t), dtype=bool)), s_new, -jnp.inf)
    p = jax.nn.softmax(jnp.concatenate([s_past, s_new], axis=-1), axis=-1).astype(v.dtype)
    past = k_past.shape[1]
    return jnp.einsum('bhqk,bkhd->bqhd', p[..., :past], v_past) + jnp.einsum('bhqk,bkhd->bqhd', p[..., past:], v)


def rwkv_mix(xn, x_last, s0, mu, w_in, w0, w1, w2, a0, a1, a2, k_k, k_a, r_k, ln_g, ln_b, w_out):
    nb, t, d = xn.shape
    hd = (nb, t, N_HEADS, HEAD_DIM)
    f32 = jnp.float32
    x_prev = jnp.concatenate([x_last[:, None, :].astype(xn.dtype), xn[:, :-1]], axis=1)
    xm = xn[:, :, None, :] + (x_prev - xn)[:, :, None, :] * mu
    rkvg = jnp.einsum('btgd,dge->btge', xm[:, :, :4], w_in.reshape(d, 4, d))
    r, k, v, g = rkvg[:, :, 0], rkvg[:, :, 1], rkvg[:, :, 2], rkvg[:, :, 3]
    xw, xa = xm[:, :, 4], xm[:, :, 5]
    w_log = -jax.nn.softplus(-(w0 + jnp.tanh(xw @ w1) @ w2).astype(f32)) - 0.5
    decay = jnp.exp(-jnp.exp(w_log)).reshape(hd)
    a = jax.nn.sigmoid((a0 + (xa @ a1) @ a2).astype(f32))
    kk = (k * k_k).astype(f32).reshape(hd)
    kk = kk / jnp.maximum(jnp.sqrt(jnp.sum(kk * kk, axis=-1, keepdims=True)), 1e-12)
    k = (k.astype(f32) * (1.0 + (a - 1.0) * k_a)).reshape(hd)
    a = a.reshape(hd)
    r = r.astype(f32).reshape(hd)
    v = v.astype(f32).reshape(hd)

    def step(state, inp):
        r_t, w_t, k_t, v_t, a_t, b_t = inp
        sa = jnp.einsum('bhij,bhj->bhi', state, a_t)
        state = state * w_t[:, :, None, :] + sa[..., None] * b_t[:, :, None, :] + v_t[..., None] * k_t[:, :, None, :]
        return state, jnp.einsum('bhij,bhj->bhi', state, r_t)

    seq = tuple(z.transpose(1, 0, 2, 3) for z in (r, decay, k, v, -kk, kk * a))
    s_fin, y = lax.scan(step, s0.astype(f32), seq)
    y = y.transpose(1, 0, 2, 3)
    mean = jnp.mean(y, axis=-1, keepdims=True)
    var = jnp.mean(jnp.square(y - mean), axis=-1, keepdims=True)
    y = ((y - mean) * lax.rsqrt(var + GN_EPS)).reshape(nb, t, d) * ln_g + ln_b
    bonus = jnp.sum(r * k * r_k, axis=-1, keepdims=True) * v
    y = y + bonus.reshape(nb, t, d)
    return gated_out(y, g, w_out), s_fin, xn[:, -1]


def moba_prompt(q, k, v):
    nb, s = q.shape[:2]
    f32 = jnp.float32
    scale = HEAD_DIM ** -0.5
    n_blk = -(-s // MOBA_BLOCK)
    pad = n_blk * MOBA_BLOCK - s

    def to_blocks(z):
        z = jnp.pad(z, ((0, 0), (0, pad), (0, 0), (0, 0))).transpose(0, 2, 1, 3)
        return z.reshape(nb, N_HEADS, n_blk, MOBA_BLOCK, HEAD_DIM)

    kb, vb = to_blocks(k), to_blocks(v)
    k_mean = kb.astype(f32).mean(axis=3)
    n_sel = min(MOBA_TOPK, (s - 1) // MOBA_BLOCK)
    slopes = alibi_slopes()
    nqb = s // Q_BLOCK
    hidx = jnp.arange(N_HEADS)[:, None, None]
    blk_ar = jnp.arange(MOBA_BLOCK)

    def block(item):
        bi, j = item // nqb, item % nqb
        start = j * Q_BLOCK
        qb = lax.dynamic_slice(q, (bi, start, 0, 0), (1, Q_BLOCK, N_HEADS, HEAD_DIM))[0].transpose(1, 0, 2)
        qpos = start + jnp.arange(Q_BLOCK)
        own = start // MOBA_BLOCK
        k_seq, v_seq = kb[bi], vb[bi]
        k_own = lax.dynamic_index_in_dim(k_seq, own, axis=1, keepdims=False)
        v_own = lax.dynamic_index_in_dim(v_seq, own, axis=1, keepdims=False)
        kpos_own = own * MOBA_BLOCK + blk_ar
        s_own = jnp.einsum('hqd,hnd->hqn', qb, k_own).astype(f32) * scale - slopes[:, None, None] * (qpos[:, None] - kpos_own[None, :]).astype(f32)
        s_own = jnp.where(kpos_own[None, :] <= qpos[:, None], s_own, -jnp.inf)
        if n_sel == 0:
            p = jax.nn.softmax(s_own, axis=-1).astype(v.dtype)
            return jnp.einsum('hqn,hnd->qhd', p, v_own)
        gate = jnp.einsum('hqd,hnd->hqn', qb.astype(f32), k_mean[bi])
        gate = jnp.where(jnp.arange(n_blk) < own, gate, -jnp.inf)
        idx = lax.top_k(gate, n_sel)[1]
        valid = idx < own
        k_sel, v_sel = k_seq[hidx, idx], v_seq[hidx, idx]
        kpos_sel = idx[..., None] * MOBA_BLOCK + blk_ar
        s_sel = jnp.einsum('hqd,hqsnd->hqsn', qb, k_sel).astype(f32) * scale - slopes[:, None, None, None] * (qpos[None, :, None, None] - kpos_sel).astype(f32)
        s_sel = jnp.where(valid[..., None], s_sel, -jnp.inf).reshape(N_HEADS, Q_BLOCK, n_sel * MOBA_BLOCK)
        p = jax.nn.softmax(jnp.concatenate([s_sel, s_own], axis=-1), axis=-1).astype(v.dtype)
        p_sel = p[..., :n_sel * MOBA_BLOCK].reshape(N_HEADS, Q_BLOCK, n_sel, MOBA_BLOCK)
        return jnp.einsum('hqsn,hqsnd->qhd', p_sel, v_sel) + jnp.einsum('hqn,hnd->qhd', p[..., n_sel * MOBA_BLOCK:], v_own)

    o = lax.map(block, jnp.arange(nb * nqb))
    return o.reshape(nb, s, N_HEADS, HEAD_DIM)


def moba_sample(q, k, v, pool_k, pool_v, layer, page_table):
    nb, t = q.shape[:2]
    f32 = jnp.float32
    scale = HEAD_DIM ** -0.5
    slopes = alibi_slopes()
    qh = q.transpose(0, 2, 1, 3)
    qpos = PAST_LEN + np.arange(t)
    own = qpos // MOBA_BLOCK
    own_pos = own[:, None] * MOBA_BLOCK + np.arange(MOBA_BLOCK)
    in_past = (own_pos < PAST_LEN)[None, :, :, None, None]
    past_pos = jnp.broadcast_to(jnp.asarray(np.minimum(own_pos, PAST_LEN - 1), dtype=jnp.int32), (nb, t, MOBA_BLOCK))
    new_pos = np.clip(own_pos - PAST_LEN, 0, t - 1)
    k_own = jnp.where(in_past, paged_rows(pool_k, layer, page_table, past_pos), k[:, new_pos])
    v_own = jnp.where(in_past, paged_rows(pool_v, layer, page_table, past_pos), v[:, new_pos])
    dist_own = jnp.asarray((qpos[:, None] - own_pos).astype(np.float32))
    s_own = jnp.einsum('bhqd,bqnhd->bhqn', qh, k_own).astype(f32) * scale - slopes[:, None, None] * dist_own
    s_own = jnp.where(own_pos <= qpos[:, None], s_own, -jnp.inf)
    n_full = PAST_LEN // MOBA_BLOCK
    n_sel = min(MOBA_TOPK, n_full)
    if n_sel == 0:
        p = jax.nn.softmax(s_own, axis=-1).astype(v.dtype)
        return jnp.einsum('bhqn,bqnhd->bqhd', p, v_own)
    k_past = gather_past(pool_k, layer, page_table)[:, :n_full * MOBA_BLOCK]
    k_mean = k_past.astype(f32).reshape(nb, n_full, MOBA_BLOCK, N_HEADS, HEAD_DIM).mean(axis=2)
    gate = jnp.einsum('bhqd,bnhd->bhqn', qh.astype(f32), k_mean)
    gate = jnp.where(np.arange(n_full)[None, :] < own[:, None], gate, -jnp.inf)
    idx = lax.top_k(gate, n_sel)[1]
    valid = idx < jnp.asarray(own, dtype=jnp.int32)[:, None]
    kpos_sel = idx[..., None] * MOBA_BLOCK + jnp.arange(MOBA_BLOCK)
    hidx = jnp.arange(N_HEADS)[None, :, None, None, None]
    k_sel = paged_rows(pool_k, layer, page_table, kpos_sel, hidx)
    v_sel = paged_rows(pool_v, layer, page_table, kpos_sel, hidx)
    dist_sel = (jnp.asarray(qpos, dtype=jnp.int32)[:, None, None] - kpos_sel).astype(f32)
    s_sel = jnp.einsum('bhqd,bhqsnd->bhqsn', qh, k_sel).astype(f32) * scale - slopes[:, None, None, None] * dist_sel
    s_sel = jnp.where(valid[..., None], s_sel, -jnp.inf).reshape(nb, N_HEADS, t, n_sel * MOBA_BLOCK)
    p = jax.nn.softmax(jnp.concatenate([s_sel, s_own], axis=-1), axis=-1).astype(v.dtype)
    p_sel = p[..., :n_sel * MOBA_BLOCK].reshape(nb, N_HEADS, t, n_sel, MOBA_BLOCK)
    return jnp.einsum('bhqsn,bhqsnd->bqhd', p_sel, v_sel) + jnp.einsum('bhqn,bqnhd->bqhd', p[..., n_sel * MOBA_BLOCK:], v_own)


def setup_inputs(seed: int = 0) -> dict:
    key = jax.random.key(seed)
    ks = iter(jax.random.split(key, 64))
    nrm = lambda shape, scale=1.0: scale * jax.random.normal(next(ks), shape, jnp.float32)
    d, h, dh = D_MODEL, N_HEADS, HEAD_DIM
    ws = d ** -0.5
    pool = (N_POOL, PAGE_SIZE, h, dh)
    page_table = jax.random.permutation(next(ks), N_POOL)[:DEC_BATCH * N_PAGES].reshape(DEC_BATCH, N_PAGES).astype(jnp.int32)
    return {
        'x_prompt': nrm((BATCH, SEQ, d)),
        'x_sample': nrm((DEC_BATCH, DEC_SEQ, d)),
        'cache_fox_k': nrm((N_FOX,) + pool),
        'cache_fox_v': nrm((N_FOX,) + pool),
        'cache_fox_logf': jax.nn.log_sigmoid(FORGET_BIAS + nrm((N_FOX, N_POOL, PAGE_SIZE, h))),
        'state_rwkv_wkv': nrm((N_RWKV, DEC_BATCH, h, dh, dh), 0.3),
        'state_rwkv_shift': nrm((N_RWKV, DEC_BATCH, d)),
        'cache_moba_k': nrm((N_MOBA,) + pool),
        'cache_moba_v': nrm((N_MOBA,) + pool),
        'page_table': page_table,
        'norm_g': 1.0 + nrm((DEPTH, d), 0.02),
        'fox_w_in': nrm((N_FOX, d, 4 * d + h), ws),
        'fox_b_f': FORGET_BIAS + nrm((N_FOX, h), 0.1),
        'fox_q_g': 1.0 + nrm((N_FOX, dh), 0.02),
        'fox_k_g': 1.0 + nrm((N_FOX, dh), 0.02),
        'fox_w_out': nrm((N_FOX, d, d), ws),
        'rwkv_mu': jax.random.uniform(next(ks), (N_RWKV, 6, d), jnp.float32),
        'rwkv_w_in': nrm((N_RWKV, d, 4 * d), ws),
        'rwkv_w0': jax.random.uniform(next(ks), (N_RWKV, d), jnp.float32, minval=-6.0, maxval=0.0),
        'rwkv_w1': nrm((N_RWKV, d, RWKV_LORA_W), ws),
        'rwkv_w2': nrm((N_RWKV, RWKV_LORA_W, d), 0.1),
        'rwkv_a0': nrm((N_RWKV, d), 0.1),
        'rwkv_a1': nrm((N_RWKV, d, RWKV_LORA_A), ws),
        'rwkv_a2': nrm((N_RWKV, RWKV_LORA_A, d), 0.1),
        'rwkv_k_k': 0.85 + nrm((N_RWKV, d), 0.02),
        'rwkv_k_a': 1.0 + nrm((N_RWKV, d), 0.02),
        'rwkv_r_k': nrm((N_RWKV, h, dh), 0.1),
        'rwkv_ln_g': 1.0 + nrm((N_RWKV, d), 0.02),
        'rwkv_ln_b': nrm((N_RWKV, d), 0.02),
        'rwkv_w_out': nrm((N_RWKV, d, d), ws),
        'moba_w_in': nrm((N_MOBA, d, 4 * d), ws),
        'moba_q_g': 1.0 + nrm((N_MOBA, dh), 0.02),
        'moba_k_g': 1.0 + nrm((N_MOBA, dh), 0.02),
        'moba_w_out': nrm((N_MOBA, d, d), ws),
    }


def reference(x_prompt, x_sample, cache_fox_k, cache_fox_v, cache_fox_logf, state_rwkv_wkv, state_rwkv_shift,
              cache_moba_k, cache_moba_v, page_table, norm_g, fox_w_in, fox_b_f, fox_q_g, fox_k_g, fox_w_out,
              rwkv_mu, rwkv_w_in, rwkv_w0, rwkv_w1, rwkv_w2, rwkv_a0, rwkv_a1, rwkv_a2, rwkv_k_k, rwkv_k_a,
              rwkv_r_k, rwkv_ln_g, rwkv_ln_b, rwkv_w_out, moba_w_in, moba_q_g, moba_k_g, moba_w_out):
    xp, xs = x_prompt, x_sample
    fk_p, fv_p, flf_p, fk_s, fv_s, flf_s = [], [], [], [], [], []
    rw_p, rsh_p, rw_s, rsh_s = [], [], [], []
    mk_p, mv_p, mk_s, mv_s = [], [], [], []
    for i in range(DEPTH):
        kind, j = i % N_MIXERS, i // N_MIXERS
        hp, hs = rms_norm(xp, norm_g[i]), rms_norm(xs, norm_g[i])
        if kind == 0:
            qp, kp, vp, gp, fp = attn_project(hp, fox_w_in[j], fox_q_g[j], fox_k_g[j])
            qs, ks_, vs, gs, fs = attn_project(hs, fox_w_in[j], fox_q_g[j], fox_k_g[j])
            lfp = jax.nn.log_sigmoid((fp + fox_b_f[j]).astype(jnp.float32))
            lfs = jax.nn.log_sigmoid((fs + fox_b_f[j]).astype(jnp.float32))
            dp = gated_out(fox_prompt(qp, kp, vp, lfp), gp, fox_w_out[j])
            ds = gated_out(fox_sample(qs, ks_, vs, lfs, cache_fox_k, cache_fox_v, cache_fox_logf, j, page_table), gs, fox_w_out[j])
            fk_p.append(kp); fv_p.append(vp); flf_p.append(lfp)
            fk_s.append(ks_); fv_s.append(vs); flf_s.append(lfs)
        elif kind == 1:
            wts = (rwkv_mu[j], rwkv_w_in[j], rwkv_w0[j], rwkv_w1[j], rwkv_w2[j], rwkv_a0[j], rwkv_a1[j], rwkv_a2[j],
                   rwkv_k_k[j], rwkv_k_a[j], rwkv_r_k[j], rwkv_ln_g[j], rwkv_ln_b[j], rwkv_w_out[j])
            zero_shift = jnp.zeros((hp.shape[0], D_MODEL), hp.dtype)
            zero_state = jnp.zeros((hp.shape[0], N_HEADS, HEAD_DIM, HEAD_DIM), jnp.float32)
            dp, sp_fin, shp = rwkv_mix(hp, zero_shift, zero_state, *wts)
            ds, ss_fin, shs = rwkv_mix(hs, state_rwkv_shift[j], state_rwkv_wkv[j], *wts)
            rw_p.append(sp_fin); rsh_p.append(shp); rw_s.append(ss_fin); rsh_s.append(shs)
        else:
            qp, kp, vp, gp, _ = attn_project(hp, moba_w_in[j], moba_q_g[j], moba_k_g[j])
            qs, ks_, vs, gs, _ = attn_project(hs, moba_w_in[j], moba_q_g[j], moba_k_g[j])
            dp = gated_out(moba_prompt(qp, kp, vp), gp, moba_w_out[j])
            ds = gated_out(moba_sample(qs, ks_, vs, cache_moba_k, cache_moba_v, j, page_table), gs, moba_w_out[j])
            mk_p.append(kp); mv_p.append(vp); mk_s.append(ks_); mv_s.append(vs)
        xp = xp + dp
        xs = xs + ds
    fox_k_p, fox_v_p, fox_logf_p = jnp.stack(fk_p), jnp.stack(fv_p), jnp.stack(flf_p)
    fox_k_s, fox_v_s, fox_logf_s = jnp.stack(fk_s), jnp.stack(fv_s), jnp.stack(flf_s)
    rwkv_wkv_p, rwkv_shift_p = jnp.stack(rw_p), jnp.stack(rsh_p)
    rwkv_wkv_s, rwkv_shift_s = jnp.stack(rw_s), jnp.stack(rsh_s)
    moba_k_p, moba_v_p = jnp.stack(mk_p), jnp.stack(mv_p)
    moba_k_s, moba_v_s = jnp.stack(mk_s), jnp.stack(mv_s)
    return (xp, xs, fox_k_p, fox_v_p, fox_logf_p, fox_k_s, fox_v_s, fox_logf_s, rwkv_wkv_p, rwkv_shift_p, rwkv_wkv_s, rwkv_shift_s, moba_k_p, moba_v_p, moba_k_s, moba_v_s)
```

```python
import functools

import numpy as np
import jax
import jax.numpy as jnp
from jax import lax
from jax.experimental import pallas as pl
from jax.experimental.pallas import tpu as pltpu

D_MODEL = 1024
N_HEADS = 16
HEAD_DIM = 64
LANES = 128
HEAD_PAIRS = N_HEADS // 2
PAGE_SIZE = 128
MOBA_BLOCK = 256
MOBA_TOPK = 3
RMS_EPS = 1e-6
GN_EPS = 64e-5
NEG_BIG = -1e30
VMEM_LIMIT = 56 * 1024 * 1024

f32 = jnp.float32
bf16 = jnp.bfloat16


def _cparams(*sem):
    return pltpu.CompilerParams(dimension_semantics=sem, vmem_limit_bytes=VMEM_LIMIT)


def _split3(x):
    p1 = x.astype(bf16)
    r1 = x - p1.astype(f32)
    p2 = r1.astype(bf16)
    p3 = (r1 - p2.astype(f32)).astype(bf16)
    return p1, p2, p3


def _dot(a, b):
    return jnp.dot(a, b, preferred_element_type=f32)


def _dot_nt(a, b, precision=None):
    return lax.dot_general(a, b, (((1,), (1,)), ((), ())), preferred_element_type=f32, precision=precision)


def _dot_exact_lhs(a_bf16, x):
    p1, p2, p3 = _split3(x)
    return _dot(a_bf16, p1) + _dot(a_bf16, p2) + _dot(a_bf16, p3)


def _log_sigmoid(z):
    return jnp.minimum(z, 0.0) - jnp.log1p(jnp.exp(-jnp.abs(z)))


def _first_head_lanes():
    return lax.broadcasted_iota(jnp.int32, (1, LANES), 1) < HEAD_DIM


def _per_head_sum(x):
    first = _first_head_lanes()
    s0 = jnp.sum(jnp.where(first, x, 0.0), axis=-1, keepdims=True)
    s1 = jnp.sum(jnp.where(first, 0.0, x), axis=-1, keepdims=True)
    return jnp.where(first, s0, s1)


def _attn_proj_kernel(x_ref, ng_ref, w_ref, wf_ref, bf_ref, qg_ref, kg_ref,
                      q_ref, k_ref, v_ref, g_ref, lf_ref):
    x = x_ref[...]
    ms = jnp.mean(x * x, axis=-1, keepdims=True)
    hn = (x * lax.rsqrt(ms + RMS_EPS) * ng_ref[...]).astype(bf16)
    d = D_MODEL
    for sec, (out_ref, gain_ref) in enumerate(((q_ref, qg_ref), (k_ref, kg_ref))):
        p = _dot(hn, w_ref[:, sec * d:(sec + 1) * d])
        for c in range(d // LANES):
            blk = p[:, c * LANES:(c + 1) * LANES]
            msq = _per_head_sum(blk * blk) * (1.0 / HEAD_DIM)
            out_ref[:, c * LANES:(c + 1) * LANES] = blk * lax.rsqrt(msq + RMS_EPS) * gain_ref[...]
    v_ref[...] = _dot(hn, w_ref[:, 2 * d:3 * d])
    g_ref[...] = _dot(hn, w_ref[:, 3 * d:4 * d])
    lf = _log_sigmoid(_dot(hn, wf_ref[...]) + bf_ref[...])
    lf_ref[...] = lf[:, :N_HEADS]


def _row_tile(n):
    for t in (512, 256, 128, 64, 32, 16, 8):
        if n % t == 0:
            return t
    raise ValueError(f"row count {n} is not a multiple of 8")


def attn_project(x2d, norm_g, w_in, q_g, k_g, b_f=None):
    n, d = x2d.shape
    tm = _row_tile(n)
    w_main = w_in[:, :4 * d].astype(bf16)
    if b_f is None:
        wf = jnp.zeros((d, LANES), bf16)
        bfp = jnp.zeros((1, LANES), f32)
    else:
        wf = jnp.pad(w_in[:, 4 * d:], ((0, 0), (0, LANES - N_HEADS))).astype(bf16)
        bfp = jnp.pad(b_f.reshape(1, N_HEADS), ((0, 0), (0, LANES - N_HEADS)))
    qg = jnp.tile(q_g.reshape(1, HEAD_DIM), (1, 2))
    kg = jnp.tile(k_g.reshape(1, HEAD_DIM), (1, 2))
    row = lambda i: (i, 0)
    const = lambda i: (0, 0)
    big = jax.ShapeDtypeStruct((n, d), f32)
    return pl.pallas_call(
        _attn_proj_kernel,
        grid=(n // tm,),
        in_specs=[pl.BlockSpec((tm, d), row), pl.BlockSpec((1, d), const),
                  pl.BlockSpec((d, 4 * d), const), pl.BlockSpec((d, LANES), const),
                  pl.BlockSpec((1, LANES), const), pl.BlockSpec((1, LANES), const),
                  pl.BlockSpec((1, LANES), const)],
        out_specs=[pl.BlockSpec((tm, d), row)] * 4 + [pl.BlockSpec((tm, N_HEADS), row)],
        out_shape=[big, big, big, big, jax.ShapeDtypeStruct((n, N_HEADS), f32)],
        compiler_params=_cparams("parallel"),
        name="attn_proj",
    )(x2d, norm_g.reshape(1, d), w_main, wf, bfp, qg, kg)


def _out_proj_kernel(x_ref, o_ref, g_ref, w_ref, y_ref):
    g = g_ref[...]
    a = o_ref[...] * (g * jax.nn.sigmoid(g))
    y_ref[...] = x_ref[...] + _dot(a.astype(bf16), w_ref[...])


def out_project(x2d, o2d, g2d, w_out):
    n, d = x2d.shape
    tm = _row_tile(n)
    row = lambda i: (i, 0)
    return pl.pallas_call(
        _out_proj_kernel,
        grid=(n // tm,),
        in_specs=[pl.BlockSpec((tm, d), row)] * 3 + [pl.BlockSpec((d, d), lambda i: (0, 0))],
        out_specs=pl.BlockSpec((tm, d), row),
        out_shape=jax.ShapeDtypeStruct((n, d), f32),
        compiler_params=_cparams("parallel"),
        name="out_proj",
    )(x2d, o2d, g2d, w_out.astype(bf16))


ATT_TILE = 256
PEN_LANE0 = 16


def _lane_ids():
    lane = lax.broadcasted_iota(jnp.int32, (1, LANES), 1)
    return lane, lane & (HEAD_DIM - 1)


def _fox_extras(lf_ref, cp_ref, seq):
    hp = pl.program_id(1)
    lane, lmod = _lane_ids()
    hrow = lax.broadcasted_iota(jnp.int32, (N_HEADS, LANES), 0)
    hlane = lax.broadcasted_iota(jnp.int32, (N_HEADS, LANES), 1)
    place = (((hrow == 2 * hp) & (hlane >= HEAD_DIM) & (hlane < HEAD_DIM + 6))
             | ((hrow == 2 * hp + 1) & (hlane < 6)))
    place = jnp.where(place, 1.0, 0.0).astype(bf16)
    l1, l2, l3 = _split3(lf_ref[0])
    lfp = _dot(l1, place) + _dot(l2, place) + _dot(l3, place)
    r = lax.broadcasted_iota(jnp.int32, (ATT_TILE, ATT_TILE), 0)
    c = lax.broadcasted_iota(jnp.int32, (ATT_TILE, ATT_TILE), 1)
    tri = jnp.where(r >= c, 1.0, 0.0).astype(bf16)
    carry = jnp.zeros((1, LANES), f32)
    for n in range(seq // ATT_TILE):
        cb = _dot_exact_lhs(tri, lfp[n * ATT_TILE:(n + 1) * ATT_TILE]) + carry
        carry = cb[ATT_TILE - 1:ATT_TILE, :]
        cp_ref[n * ATT_TILE:(n + 1) * ATT_TILE, :] = cb
    cum = cp_ref[...]
    p1 = cum.astype(bf16).astype(f32)
    r1 = cum - p1
    p2 = r1.astype(bf16).astype(f32)
    p3 = (r1 - p2).astype(bf16).astype(f32)
    third = jnp.where(lmod >= 3, lmod - 3, lmod)
    parts = jnp.where(third == 0, p1, jnp.where(third == 1, p2, p3))
    qx = jnp.where(lmod < 3, parts, jnp.where(lmod < 6, 1.0, 0.0))
    kx = jnp.where(lmod < 3, 1.0, jnp.where(lmod < 6, -parts, 0.0))
    return qx, kx


def _moba_extras(sp_ref, seq):
    lane, lmod = _lane_ids()
    sp = sp_ref[0, 0:1, :]
    pos = lax.broadcasted_iota(jnp.int32, (seq, LANES), 0)
    hi = (pos >> 7).astype(f32)
    lo = (pos & 127).astype(f32)
    blk = pos >> 8
    qx = jnp.where(lmod < 6, sp, jnp.where(lmod < 9, -hi, jnp.where(lmod < 12, -lo, 0.0)))
    onehot = jnp.where((lmod >= PEN_LANE0) & (lmod < PEN_LANE0 + 8) & (blk == lmod - PEN_LANE0), 1.0, 0.0)
    kx = jnp.where(lmod < 3, 128.0 * hi,
                   jnp.where(lmod < 6, lo,
                             jnp.where(lmod < 9, 128.0 * sp, jnp.where(lmod < 12, sp, onehot))))
    return qx, kx


def _moba_penalty(q_own, kmr_ref, own, lane0):
    lane = lax.broadcasted_iota(jnp.int32, (1, LANES), 1)
    gate = _dot_nt(q_own, kmr_ref[...], precision=lax.Precision.HIGHEST)
    cand = (lane >= lane0) & (lane < lane0 + own)
    g = jnp.where(cand, gate, -3e38)
    sel = jnp.zeros(gate.shape, jnp.bool_)
    for _ in range(MOBA_TOPK):
        mx = jnp.max(g, axis=-1, keepdims=True)
        idx = jnp.min(jnp.where(g == mx, lane, 4 * LANES), axis=-1, keepdims=True)
        pick = (lane == idx) & (mx > -3e38)
        sel = sel | pick
        g = jnp.where(pick, -3e38, g)
    return jnp.where(cand & jnp.logical_not(sel), NEG_BIG, 0.0)


def _flash_kernel(q_ref, k_ref, v_ref, aux_ref, o_ref, qx_ref, ka0_ref, ka1_ref, vb_ref, cp_ref,
                  kmr0_ref, kmr1_ref, *, mode, seq):
    first = _first_head_lanes()
    scale = HEAD_DIM ** -0.5
    if mode == "fox":
        qx, kx = _fox_extras(aux_ref, cp_ref, seq)
    else:
        qx, kx = _moba_extras(aux_ref, seq)
        nblk = seq // MOBA_BLOCK
        kmr0_ref[...] = jnp.zeros((LANES, LANES), f32)
        kmr1_ref[...] = jnp.zeros((LANES, LANES), f32)
        for n in range(nblk):
            km = jnp.sum(k_ref[0, n * MOBA_BLOCK:(n + 1) * MOBA_BLOCK, :], axis=0, keepdims=True) * (1.0 / MOBA_BLOCK)
            r0 = HEAD_DIM + PEN_LANE0 + n
            kmr0_ref[r0:r0 + 1, :] = km
            kmr1_ref[PEN_LANE0 + n:PEN_LANE0 + n + 1, :] = km
    qx_ref[...] = qx
    kk = k_ref[0]
    ka0_ref[...] = jnp.where(first, kk, kx).astype(bf16)
    ka1_ref[...] = jnp.where(first, kx, kk).astype(bf16)
    vb_ref[...] = v_ref[0].astype(bf16)

    tq = ATT_TILE
    rr = lax.broadcasted_iota(jnp.int32, (tq, tq), 0)
    cc = lax.broadcasted_iota(jnp.int32, (tq, tq), 1)
    causal = cc <= rr

    def q_tile(qi, carry_unused):
        row0 = pl.multiple_of(qi * tq, tq)
        q_raw = q_ref[0, pl.ds(row0, tq), :]
        qxt = qx_ref[pl.ds(row0, tq), :]
        outs = []
        for hh in (0, 1):
            own_lanes = first if hh == 0 else jnp.logical_not(first)
            qa = jnp.where(own_lanes, q_raw * scale, qxt)
            if mode == "moba":
                kmr_ref = kmr0_ref if hh == 0 else kmr1_ref
                lane0 = (HEAD_DIM if hh == 0 else 0) + PEN_LANE0
                qa = qa + _moba_penalty(jnp.where(own_lanes, q_raw, 0.0), kmr_ref, qi, lane0)
            qa = qa.astype(bf16)
            ka_ref = ka0_ref if hh == 0 else ka1_ref

            def tile_update(col0, carry, mask):
                m, l, acc = carry
                s = _dot_nt(qa, ka_ref[pl.ds(col0, tq), :])
                if mask:
                    s = jnp.where(causal, s, NEG_BIG)
                m_new = jnp.maximum(m, jnp.max(s, axis=-1, keepdims=True))
                alpha = jnp.exp(m - m_new)
                p = jnp.exp(s - m_new)
                l = alpha * l + jnp.sum(p, axis=-1, keepdims=True)
                acc = alpha * acc + _dot(p.astype(bf16), vb_ref[pl.ds(col0, tq), :])
                return m_new, l, acc

            init = (jnp.full((tq, 1), NEG_BIG, f32), jnp.zeros((tq, 1), f32), jnp.zeros((tq, LANES), f32))
            carry = tile_update(row0, init, True)
            carry = lax.fori_loop(
                0, qi, lambda kj, cr: tile_update(pl.multiple_of(kj * tq, tq), cr, False), carry)
            _, l, acc = carry
            outs.append(acc / l)
        o_ref[0, pl.ds(row0, tq), :] = jnp.where(first, outs[0], outs[1])
        return carry_unused

    lax.fori_loop(0, seq // tq, q_tile, 0)


def _alibi_piece_table():
    slopes = jnp.asarray(2.0 ** (-8.0 * np.arange(1, N_HEADS + 1, dtype=np.float32) / N_HEADS), f32)
    p1, p2, p3 = (p.astype(f32) for p in _split3(slopes))
    pieces = jnp.stack([p1, p2, p3], axis=-1)
    lane = np.arange(LANES)
    lmod = lane % HEAD_DIM
    owner = np.where(lane < HEAD_DIM, 1, 0)
    tab = pieces[2 * np.arange(HEAD_PAIRS)[:, None] + owner[None, :], (lmod % 3)[None, :]]
    tab = jnp.where((lmod < 12)[None, :], tab, 0.0)
    return jnp.broadcast_to(tab[:, None, :], (HEAD_PAIRS, 8, LANES))


def prompt_attention(q, k, v, logf=None):
    nb, seq, d = q.shape
    assert seq % ATT_TILE == 0 and seq // MOBA_BLOCK <= 8 and seq <= 128 * 256
    mode = "moba" if logf is None else "fox"
    qkv_spec = pl.BlockSpec((1, seq, LANES), lambda b, hp: (b, 0, hp))
    if mode == "fox":
        aux, aux_spec = logf, pl.BlockSpec((1, seq, N_HEADS), lambda b, hp: (b, 0, 0))
    else:
        aux, aux_spec = _alibi_piece_table(), pl.BlockSpec((1, 8, LANES), lambda b, hp: (hp, 0, 0))
    return pl.pallas_call(
        functools.partial(_flash_kernel, mode=mode, seq=seq),
        grid=(nb, HEAD_PAIRS),
        in_specs=[qkv_spec, qkv_spec, qkv_spec, aux_spec],
        out_specs=qkv_spec,
        out_shape=jax.ShapeDtypeStruct((nb, seq, d), f32),
        scratch_shapes=[pltpu.VMEM((seq, LANES), f32), pltpu.VMEM((seq, LANES), bf16),
                        pltpu.VMEM((seq, LANES), bf16), pltpu.VMEM((seq, LANES), bf16),
                        pltpu.VMEM((seq, LANES), f32), pltpu.VMEM((LANES, LANES), f32),
                        pltpu.VMEM((LANES, LANES), f32)],
        compiler_params=_cparams("parallel", "parallel"),
        name=f"{mode}_prompt_attn",
    )(q, k, v, aux)


def _rms_kernel(x_ref, ng_ref, h_ref):
    x = x_ref[...]
    ms = jnp.mean(x * x, axis=-1, keepdims=True)
    h_ref[...] = x * lax.rsqrt(ms + RMS_EPS) * ng_ref[...]


def rms_norm_rows(x2d, norm_g):
    n, d = x2d.shape
    tm = _row_tile(n)
    row = lambda i: (i, 0)
    return pl.pallas_call(
        _rms_kernel, grid=(n // tm,),
        in_specs=[pl.BlockSpec((tm, d), row), pl.BlockSpec((1, d), lambda i: (0, 0))],
        out_specs=pl.BlockSpec((tm, d), row),
        out_shape=jax.ShapeDtypeStruct((n, d), f32),
        compiler_params=_cparams("parallel"), name="rms_norm",
    )(x2d, norm_g.reshape(1, d))


def _rwkv_proj_kernel(h_ref, hp_ref, mu_ref, w_ref, w0_ref, w1_ref, w2_ref, a0_ref, a1_ref, a2_ref,
                      kk_ref, ka_ref, rk_ref,
                      r_ref, dec_ref, k_ref, v_ref, na_ref, b_ref, g_ref, bonus_ref):
    d = D_MODEL
    h = h_ref[...]
    delta = hp_ref[...] - h

    def mixed(i):
        return (h + delta * mu_ref[i:i + 1, :]).astype(bf16)

    r = _dot(mixed(0), w_ref[:, 0:d])
    k = _dot(mixed(1), w_ref[:, d:2 * d])
    v = _dot(mixed(2), w_ref[:, 2 * d:3 * d])
    g_ref[...] = _dot(mixed(3), w_ref[:, 3 * d:4 * d])
    ww = w0_ref[...] + _dot(jnp.tanh(_dot(mixed(4), w1_ref[...])).astype(bf16), w2_ref[...])
    dec_ref[...] = jnp.exp(-jnp.exp(_log_sigmoid(ww) - 0.5))
    aa = jax.nn.sigmoid(a0_ref[...] + _dot(_dot(mixed(5), a1_ref[...]).astype(bf16), a2_ref[...]))
    k2 = k * (1.0 + (aa - 1.0) * ka_ref[...])
    kkv = k * kk_ref[...]
    rkk = r * k2 * rk_ref[...]
    r_ref[...] = r
    k_ref[...] = k2
    v_ref[...] = v
    for c in range(d // LANES):
        sl = slice(c * LANES, (c + 1) * LANES)
        kc = kkv[:, sl]
        kn = kc / jnp.maximum(jnp.sqrt(_per_head_sum(kc * kc)), 1e-12)
        na_ref[:, sl] = -kn
        b_ref[:, sl] = kn * aa[:, sl]
        bonus_ref[:, sl] = _per_head_sum(rkk[:, sl]) * v[:, sl]


def rwkv_project(h2d, hprev2d, mu, w_in, w0, w1, w2, a0, a1, a2, k_k, k_a, r_k):
    n, d = h2d.shape
    tm = min(_row_tile(n), 256)
    row = lambda i: (i, 0)
    const = lambda i: (0, 0)
    lw, la = w1.shape[1], a1.shape[1]
    vec = lambda z: z.reshape(1, d)
    big = jax.ShapeDtypeStruct((n, d), f32)
    return pl.pallas_call(
        _rwkv_proj_kernel, grid=(n // tm,),
        in_specs=[pl.BlockSpec((tm, d), row), pl.BlockSpec((tm, d), row), pl.BlockSpec((6, d), const),
                  pl.BlockSpec((d, 4 * d), const), pl.BlockSpec((1, d), const),
                  pl.BlockSpec((d, lw), const), pl.BlockSpec((lw, d), const), pl.BlockSpec((1, d), const),
                  pl.BlockSpec((d, la), const), pl.BlockSpec((la, d), const),
                  pl.BlockSpec((1, d), const), pl.BlockSpec((1, d), const), pl.BlockSpec((1, d), const)],
        out_specs=[pl.BlockSpec((tm, d), row)] * 8,
        out_shape=[big] * 8,
        compiler_params=_cparams("parallel"), name="rwkv_proj",
    )(h2d, hprev2d, mu, w_in.astype(bf16), vec(w0), w1.astype(bf16), w2.astype(bf16), vec(a0),
      a1.astype(bf16), a2.astype(bf16), vec(k_k), vec(k_a), vec(r_k))


SCAN_CHUNK = 32


def _rwkv_scan_kernel(r_ref, w_ref, k_ref, v_ref, na_ref, b_ref, s0_ref, y_ref, st_ref, *, tc):
    @pl.when(pl.program_id(1) == 0)
    def _():
        st_ref[...] = s0_ref[...]

    zeros = jnp.zeros((HEAD_DIM, LANES), f32)

    def step(t, carry):
        def sa_pass(j, sa):
            return sa + st_ref[j] * na_ref[t, pl.ds(j, 1), :]

        sa = lax.fori_loop(0, HEAD_DIM, sa_pass, zeros, unroll=8)
        vt = v_ref[t]

        def update_pass(j, y):
            s_new = (st_ref[j] * w_ref[t, pl.ds(j, 1), :] + sa * b_ref[t, pl.ds(j, 1), :]
                     + vt * k_ref[t, pl.ds(j, 1), :])
            st_ref[j] = s_new
            return y + s_new * r_ref[t, pl.ds(j, 1), :]

        y_ref[t] = lax.fori_loop(0, HEAD_DIM, update_pass, zeros, unroll=8)
        return carry

    lax.fori_loop(0, tc, step, 0)


def rwkv_scan(r, w, k, v, na, b, s0):
    nb, t, d = r.shape
    lanes = nb * N_HEADS
    assert lanes % LANES == 0
    tc = min(t, SCAN_CHUNK)
    assert t % tc == 0

    def to_scan(z):
        return z.reshape(nb, t, N_HEADS, HEAD_DIM).transpose(1, 3, 0, 2).reshape(t, HEAD_DIM, lanes)

    s0_l = s0.transpose(3, 2, 0, 1).reshape(HEAD_DIM, HEAD_DIM, lanes)
    seq_spec = pl.BlockSpec((tc, HEAD_DIM, LANES), lambda g, i: (i, 0, g))
    st_spec = pl.BlockSpec((HEAD_DIM, HEAD_DIM, LANES), lambda g, i: (0, 0, g))
    y, st = pl.pallas_call(
        functools.partial(_rwkv_scan_kernel, tc=tc),
        grid=(lanes // LANES, t // tc),
        in_specs=[seq_spec] * 6 + [st_spec],
        out_specs=[seq_spec, st_spec],
        out_shape=[jax.ShapeDtypeStruct((t, HEAD_DIM, lanes), f32),
                   jax.ShapeDtypeStruct((HEAD_DIM, HEAD_DIM, lanes), f32)],
        compiler_params=_cparams("parallel", "arbitrary"), name="rwkv_scan",
    )(*(to_scan(z) for z in (r, w, k, v, na, b)), s0_l)
    y = y.reshape(t, HEAD_DIM, nb, N_HEADS).transpose(2, 0, 3, 1).reshape(nb, t, d)
    st = st.reshape(HEAD_DIM, HEAD_DIM, nb, N_HEADS).transpose(2, 3, 1, 0)
    return y, st


def _rwkv_post_kernel(x_ref, y_ref, bonus_ref, g_ref, lng_ref, lnb_ref, w_ref, o_ref, a_ref):
    for c in range(D_MODEL // LANES):
        sl = slice(c * LANES, (c + 1) * LANES)
        y = y_ref[:, sl]
        dev = y - _per_head_sum(y) * (1.0 / HEAD_DIM)
        var = _per_head_sum(dev * dev) * (1.0 / HEAD_DIM)
        yn = dev * lax.rsqrt(var + GN_EPS) * lng_ref[:, sl] + lnb_ref[:, sl] + bonus_ref[:, sl]
        g = g_ref[:, sl]
        a_ref[:, sl] = (yn * (g * jax.nn.sigmoid(g))).astype(bf16)
    o_ref[...] = x_ref[...] + _dot(a_ref[...], w_ref[...])


def rwkv_post(x2d, y2d, bonus2d, g2d, ln_g, ln_b, w_out):
    n, d = x2d.shape
    tm = _row_tile(n)
    row = lambda i: (i, 0)
    const = lambda i: (0, 0)
    return pl.pallas_call(
        _rwkv_post_kernel, grid=(n // tm,),
        in_specs=[pl.BlockSpec((tm, d), row)] * 4 + [pl.BlockSpec((1, d), const)] * 2 + [pl.BlockSpec((d, d), const)],
        out_specs=pl.BlockSpec((tm, d), row),
        out_shape=jax.ShapeDtypeStruct((n, d), f32),
        scratch_shapes=[pltpu.VMEM((tm, d), bf16)],
        compiler_params=_cparams("parallel"), name="rwkv_post",
    )(x2d, y2d, bonus2d, g2d, ln_g.reshape(1, d), ln_b.reshape(1, d), w_out.astype(bf16))


def rwkv_layer(x, x_last, s0, norm_g, mu, w_in, w0, w1, w2, a0, a1, a2, k_k, k_a, r_k, ln_g, ln_b, w_out):
    nb, t, d = x.shape
    x2d = x.reshape(nb * t, d)
    hn = rms_norm_rows(x2d, norm_g).reshape(nb, t, d)
    hprev = jnp.concatenate([x_last[:, None, :], hn[:, :-1]], axis=1)
    r, dec, k, v, na, b, g, bonus = rwkv_project(hn.reshape(nb * t, d), hprev.reshape(nb * t, d), mu, w_in,
                                                 w0, w1, w2, a0, a1, a2, k_k, k_a, r_k.reshape(-1))
    shp = (nb, t, d)
    y, st = rwkv_scan(*(z.reshape(shp) for z in (r, dec, k, v, na, b)), s0)
    out = rwkv_post(x2d, y.reshape(nb * t, d), bonus, g, ln_g, ln_b, w_out)
    return out.reshape(nb, t, d), st, hn[:, -1]


PAGES_PER_STEP = 4


def _page_specs(pool, layer, n_arrays=PAGES_PER_STEP):
    blk = (1, 1) + pool.shape[2:]
    zeros = (0,) * (pool.ndim - 2)
    return [pl.BlockSpec(blk, functools.partial(
        lambda b, jg, pt, p: (layer, pt[b, jg * PAGES_PER_STEP + p]) + zeros, p=p)) for p in range(n_arrays)]


def _online_page_update(stats, s, vh):
    m, l, acc = stats
    m_new = jnp.maximum(m, jnp.max(s, axis=-1, keepdims=True))
    alpha = jnp.exp(m - m_new)
    p = jnp.exp(s - m_new)
    return m_new, alpha * l + jnp.sum(p, axis=-1, keepdims=True), alpha * acc + _dot(p, vh)


def _fox_decode_kernel(pt_ref, q_ref, *refs):
    k_refs = refs[0:PAGES_PER_STEP]
    v_refs = refs[PAGES_PER_STEP:2 * PAGES_PER_STEP]
    lf_refs = refs[2 * PAGES_PER_STEP:3 * PAGES_PER_STEP]
    m_ref, l_ref, acc_ref, tot_ref = refs[3 * PAGES_PER_STEP:]
    t = q_ref.shape[2]

    @pl.when(pl.program_id(1) == 0)
    def _():
        m_ref[...] = jnp.full(m_ref.shape, NEG_BIG, f32)
        l_ref[...] = jnp.zeros(l_ref.shape, f32)
        acc_ref[...] = jnp.zeros(acc_ref.shape, f32)
        tot_ref[...] = jnp.zeros(tot_ref.shape, f32)

    scale = HEAD_DIM ** -0.5
    r = lax.broadcasted_iota(jnp.int32, (PAGE_SIZE, PAGE_SIZE), 0)
    c = lax.broadcasted_iota(jnp.int32, (PAGE_SIZE, PAGE_SIZE), 1)
    upper = jnp.where(r <= c, 1.0, 0.0).astype(bf16)
    ones = jnp.ones((PAGE_SIZE, PAGE_SIZE), bf16)
    for p in range(PAGES_PER_STEP):
        l1, l2, l3 = _split3(lf_refs[p][0, 0])
        cum = tot_ref[0] + _dot(l1, upper) + _dot(l2, upper) + _dot(l3, upper)
        tot_ref[0] = tot_ref[0] + _dot(l1, ones) + _dot(l2, ones) + _dot(l3, ones)
        for h in range(N_HEADS):
            kh = k_refs[p][0, 0, :, h, :]
            vh = v_refs[p][0, 0, :, h, :]
            s = _dot_nt(q_ref[0, h], kh) * scale - cum[h:h + 1, :]
            stats = (m_ref[0, h][:, 0:1], l_ref[0, h][:, 0:1], acc_ref[0, h])
            m_new, l_new, acc_new = _online_page_update(stats, s, vh)
            m_ref[0, h] = jnp.broadcast_to(m_new, (t, HEAD_DIM))
            l_ref[0, h] = jnp.broadcast_to(l_new, (t, HEAD_DIM))
            acc_ref[0, h] = acc_new


def _fox_finish_kernel(q_ref, k_ref, v_ref, lf_ref, lft_ref, m_ref, l_ref, acc_ref, tot_ref, o_ref):
    t = q_ref.shape[2]
    scale = HEAD_DIM ** -0.5
    r = lax.broadcasted_iota(jnp.int32, (t, t), 0)
    c = lax.broadcasted_iota(jnp.int32, (t, t), 1)
    lower = jnp.where(r >= c, 1.0, 0.0).astype(bf16)
    upper = jnp.where(r <= c, 1.0, 0.0).astype(bf16)
    cn_col = _dot_exact_lhs(lower, lf_ref[0])
    l1, l2, l3 = _split3(lft_ref[0])
    cn_row = _dot(l1, upper) + _dot(l2, upper) + _dot(l3, upper)
    for h in range(N_HEADS):
        cq = cn_col[:, h:h + 1]
        s = _dot_nt(q_ref[0, h], k_ref[0, h]) * scale + cq - cn_row[h:h + 1, :]
        s = jnp.where(c <= r, s, NEG_BIG)
        m_past = m_ref[0, h][:, 0:1] + cq + tot_ref[0, h:h + 1, 0:1]
        m_tot = jnp.maximum(m_past, jnp.max(s, axis=-1, keepdims=True))
        w_past = jnp.exp(m_past - m_tot)
        p = jnp.exp(s - m_tot)
        l_tot = w_past * l_ref[0, h][:, 0:1] + jnp.sum(p, axis=-1, keepdims=True)
        o_ref[0, h] = (w_past * acc_ref[0, h] + _dot(p, v_ref[0, h])) / l_tot


def _heads_first(z2d, nb, t):
    return z2d.reshape(nb, t, N_HEADS, HEAD_DIM).transpose(0, 2, 1, 3)


def _heads_last(z4d):
    nb, _, t, _ = z4d.shape
    return z4d.transpose(0, 2, 1, 3).reshape(nb * t, D_MODEL)


def fox_decode_attention(q2d, k2d, v2d, logf2d, pool_k, pool_v, pool_lf, layer, page_table):
    nb, n_pages = page_table.shape
    t = q2d.shape[0] // nb
    assert n_pages % PAGES_PER_STEP == 0
    qh, kh, vh = (_heads_first(z, nb, t) for z in (q2d, k2d, v2d))
    lf = logf2d.reshape(nb, t, N_HEADS)
    pool_lft = jnp.swapaxes(pool_lf, 2, 3)
    head_spec = pl.BlockSpec((1, N_HEADS, t, HEAD_DIM), lambda b, jg, pt: (b, 0, 0, 0))
    tot_spec = pl.BlockSpec((1, N_HEADS, PAGE_SIZE), lambda b, jg, pt: (b, 0, 0))
    stat = jax.ShapeDtypeStruct((nb, N_HEADS, t, HEAD_DIM), f32)
    m, l, acc, tot = pl.pallas_call(
        _fox_decode_kernel,
        grid_spec=pltpu.PrefetchScalarGridSpec(
            num_scalar_prefetch=1, grid=(nb, n_pages // PAGES_PER_STEP),
            in_specs=[head_spec] + _page_specs(pool_k, layer) + _page_specs(pool_v, layer) + _page_specs(pool_lft, layer),
            out_specs=[head_spec, head_spec, head_spec, tot_spec]),
        out_shape=[stat, stat, stat, jax.ShapeDtypeStruct((nb, N_HEADS, PAGE_SIZE), f32)],
        compiler_params=_cparams("parallel", "arbitrary"), name="fox_decode",
    )(page_table, qh, *([pool_k] * PAGES_PER_STEP), *([pool_v] * PAGES_PER_STEP), *([pool_lft] * PAGES_PER_STEP))
    hs = pl.BlockSpec((1, N_HEADS, t, HEAD_DIM), lambda b: (b, 0, 0, 0))
    o = pl.pallas_call(
        _fox_finish_kernel, grid=(nb,),
        in_specs=[hs, hs, hs, pl.BlockSpec((1, t, N_HEADS), lambda b: (b, 0, 0)),
                  pl.BlockSpec((1, N_HEADS, t), lambda b: (b, 0, 0)), hs, hs, hs,
                  pl.BlockSpec((1, N_HEADS, PAGE_SIZE), lambda b: (b, 0, 0))],
        out_specs=hs, out_shape=stat,
        compiler_params=_cparams("parallel"), name="fox_decode_finish",
    )(qh, kh, vh, lf, jnp.swapaxes(lf, 1, 2), m, l, acc, tot)
    return _heads_last(o)


def _alibi_slopes_np():
    return 2.0 ** (-8.0 * np.arange(1, N_HEADS + 1, dtype=np.float32) / N_HEADS)


def _moba_decode_kernel(pt_ref, q_ref, *refs, past_len):
    k_refs = refs[0:PAGES_PER_STEP]
    v_refs = refs[PAGES_PER_STEP:2 * PAGES_PER_STEP]
    m_ref, l_ref, acc_ref, ksum_ref = refs[2 * PAGES_PER_STEP:]
    t = q_ref.shape[2]
    scale = HEAD_DIM ** -0.5
    slopes = _alibi_slopes_np()
    pages_per_block = MOBA_BLOCK // PAGE_SIZE
    page0 = pl.program_id(1) * PAGES_PER_STEP
    qpos = past_len + lax.broadcasted_iota(jnp.int32, (t, PAGE_SIZE), 0)
    lane = lax.broadcasted_iota(jnp.int32, (t, PAGE_SIZE), 1)
    for blk in range(PAGES_PER_STEP // pages_per_block):
        pages = range(blk * pages_per_block, (blk + 1) * pages_per_block)
        ksum = k_refs[pages[0]][0, 0].sum(axis=0)
        for p in pages[1:]:
            ksum = ksum + k_refs[p][0, 0].sum(axis=0)
        ksum_ref[0, blk] = ksum
        dist = [(qpos - ((page0 + p) * PAGE_SIZE + lane)).astype(f32) for p in pages]
        for h in range(N_HEADS):
            stats = (jnp.full((t, 1), NEG_BIG, f32), jnp.zeros((t, 1), f32), jnp.zeros((t, HEAD_DIM), f32))
            for i, p in enumerate(pages):
                s = _dot_nt(q_ref[0, h], k_refs[p][0, 0, :, h, :]) * scale - float(slopes[h]) * dist[i]
                stats = _online_page_update(stats, s, v_refs[p][0, 0, :, h, :])
            m_ref[0, blk, h] = jnp.broadcast_to(stats[0], (t, HEAD_DIM))
            l_ref[0, blk, h] = jnp.broadcast_to(stats[1], (t, HEAD_DIM))
            acc_ref[0, blk, h] = stats[2]


def _moba_finish_kernel(q_ref, k_ref, v_ref, m_ref, l_ref, acc_ref, ksum_ref, o_ref):
    t = q_ref.shape[2]
    n_blocks = m_ref.shape[1]
    scale = HEAD_DIM ** -0.5
    slopes = _alibi_slopes_np()
    r = lax.broadcasted_iota(jnp.int32, (t, t), 0)
    c = lax.broadcasted_iota(jnp.int32, (t, t), 1)
    lane = lax.broadcasted_iota(jnp.int32, (1, n_blocks), 1)
    for h in range(N_HEADS):
        qh = q_ref[0, h]
        kmean = ksum_ref[0, :, h, :] * (1.0 / MOBA_BLOCK)
        g = _dot_nt(qh, kmean, precision=lax.Precision.HIGHEST)
        sel = jnp.zeros(g.shape, f32)
        for _ in range(min(MOBA_TOPK, n_blocks)):
            mx = jnp.max(g, axis=-1, keepdims=True)
            idx = jnp.min(jnp.where(g == mx, lane, 4 * n_blocks), axis=-1, keepdims=True)
            pick = lane == idx
            sel = jnp.where(pick, 1.0, sel)
            g = jnp.where(pick, -3e38, g)
        s = _dot_nt(qh, k_ref[0, h]) * scale - float(slopes[h]) * (r - c).astype(f32)
        s = jnp.where(c <= r, s, NEG_BIG)
        m0 = jnp.max(s, axis=-1, keepdims=True)
        p = jnp.exp(s - m0)
        init = (m0, jnp.sum(p, axis=-1, keepdims=True), _dot(p, v_ref[0, h]))

        def merge(n, carry):
            m, l, acc = carry
            chosen = jnp.max(jnp.where(lane == n, sel, 0.0), axis=-1, keepdims=True)
            m_n = jnp.where(chosen > 0.0, m_ref[0, n, h][:, 0:1], NEG_BIG)
            m_new = jnp.maximum(m, m_n)
            a_old = jnp.exp(m - m_new)
            a_n = jnp.exp(m_n - m_new) * chosen
            return m_new, a_old * l + a_n * l_ref[0, n, h][:, 0:1], a_old * acc + a_n * acc_ref[0, n, h]

        _, l_tot, acc_tot = lax.fori_loop(0, n_blocks, merge, init)
        o_ref[0, h] = acc_tot / l_tot


def moba_decode_attention(q2d, k2d, v2d, pool_k, pool_v, layer, page_table):
    nb, n_pages = page_table.shape
    t = q2d.shape[0] // nb
    past_len = n_pages * PAGE_SIZE
    assert past_len % MOBA_BLOCK == 0 and t <= MOBA_BLOCK and n_pages % PAGES_PER_STEP == 0
    n_blocks = past_len // MOBA_BLOCK
    bps = PAGES_PER_STEP * PAGE_SIZE // MOBA_BLOCK
    qh, kh, vh = (_heads_first(z, nb, t) for z in (q2d, k2d, v2d))
    head_spec = pl.BlockSpec((1, N_HEADS, t, HEAD_DIM), lambda b, jg, pt: (b, 0, 0, 0))
    part_spec = pl.BlockSpec((1, bps, N_HEADS, t, HEAD_DIM), lambda b, jg, pt: (b, jg, 0, 0, 0))
    part = jax.ShapeDtypeStruct((nb, n_blocks, N_HEADS, t, HEAD_DIM), f32)
    m, l, acc, ksum = pl.pallas_call(
        functools.partial(_moba_decode_kernel, past_len=past_len),
        grid_spec=pltpu.PrefetchScalarGridSpec(
            num_scalar_prefetch=1, grid=(nb, n_pages // PAGES_PER_STEP),
            in_specs=[head_spec] + _page_specs(pool_k, layer) + _page_specs(pool_v, layer),
            out_specs=[part_spec, part_spec, part_spec,
                       pl.BlockSpec((1, bps, N_HEADS, HEAD_DIM), lambda b, jg, pt: (b, jg, 0, 0))]),
        out_shape=[part, part, part, jax.ShapeDtypeStruct((nb, n_blocks, N_HEADS, HEAD_DIM), f32)],
        compiler_params=_cparams("parallel", "parallel"), name="moba_decode",
    )(page_table, qh, *([pool_k] * PAGES_PER_STEP), *([pool_v] * PAGES_PER_STEP))
    hs = pl.BlockSpec((1, N_HEADS, t, HEAD_DIM), lambda b: (b, 0, 0, 0))
    ps = pl.BlockSpec((1, n_blocks, N_HEADS, t, HEAD_DIM), lambda b: (b, 0, 0, 0, 0))
    o = pl.pallas_call(
        _moba_finish_kernel, grid=(nb,),
        in_specs=[hs, hs, hs, ps, ps, ps, pl.BlockSpec((1, n_blocks, N_HEADS, HEAD_DIM), lambda b: (b, 0, 0, 0))],
        out_specs=hs, out_shape=jax.ShapeDtypeStruct((nb, N_HEADS, t, HEAD_DIM), f32),
        compiler_params=_cparams("parallel"), name="moba_decode_finish",
    )(qh, kh, vh, m, l, acc, ksum)
    return _heads_last(o)


N_MIXERS = 3


def kernel(x_prompt, x_sample, cache_fox_k, cache_fox_v, cache_fox_logf, state_rwkv_wkv, state_rwkv_shift, cache_moba_k, cache_moba_v, page_table, norm_g, fox_w_in, fox_b_f, fox_q_g, fox_k_g, fox_w_out, rwkv_mu, rwkv_w_in, rwkv_w0, rwkv_w1, rwkv_w2, rwkv_a0, rwkv_a1, rwkv_a2, rwkv_k_k, rwkv_k_a, rwkv_r_k, rwkv_ln_g, rwkv_ln_b, rwkv_w_out, moba_w_in, moba_q_g, moba_k_g, moba_w_out):
    nb, seq, d = x_prompt.shape
    db, dseq, _ = x_sample.shape
    depth = norm_g.shape[0]
    xp = x_prompt.reshape(nb * seq, d)
    xs = x_sample.reshape(db * dseq, d)
    heads_p = (nb, seq, N_HEADS, HEAD_DIM)
    heads_s = (db, dseq, N_HEADS, HEAD_DIM)
    outs = {name: [] for name in ("fk_p", "fv_p", "flf_p", "fk_s", "fv_s", "flf_s", "rw_p", "rsh_p", "rw_s", "rsh_s",
                                  "mk_p", "mv_p", "mk_s", "mv_s")}
    for i in range(depth):
        kind, j = i % N_MIXERS, i // N_MIXERS
        if kind == 0:
            args = (norm_g[i], fox_w_in[j], fox_q_g[j], fox_k_g[j], fox_b_f[j])
            qp, kp, vp, gp, lfp = attn_project(xp, *args)
            qs, ks, vs, gs, lfs = attn_project(xs, *args)
            op = prompt_attention(qp.reshape(nb, seq, d), kp.reshape(nb, seq, d), vp.reshape(nb, seq, d),
                                  lfp.reshape(nb, seq, N_HEADS)).reshape(nb * seq, d)
            os_ = fox_decode_attention(qs, ks, vs, lfs, cache_fox_k, cache_fox_v, cache_fox_logf, j, page_table)
            xp = out_project(xp, op, gp, fox_w_out[j])
            xs = out_project(xs, os_, gs, fox_w_out[j])
            outs["fk_p"].append(kp.reshape(heads_p)); outs["fv_p"].append(vp.reshape(heads_p))
            outs["flf_p"].append(lfp.reshape(nb, seq, N_HEADS))
            outs["fk_s"].append(ks.reshape(heads_s)); outs["fv_s"].append(vs.reshape(heads_s))
            outs["flf_s"].append(lfs.reshape(db, dseq, N_HEADS))
        elif kind == 1:
            wts = (norm_g[i], rwkv_mu[j], rwkv_w_in[j], rwkv_w0[j], rwkv_w1[j], rwkv_w2[j], rwkv_a0[j], rwkv_a1[j],
                   rwkv_a2[j], rwkv_k_k[j], rwkv_k_a[j], rwkv_r_k[j], rwkv_ln_g[j], rwkv_ln_b[j], rwkv_w_out[j])
            xp3, st_p, sh_p = rwkv_layer(xp.reshape(nb, seq, d), jnp.zeros((nb, d), f32),
                                         jnp.zeros((nb, N_HEADS, HEAD_DIM, HEAD_DIM), f32), *wts)
            xs3, st_s, sh_s = rwkv_layer(xs.reshape(db, dseq, d), state_rwkv_shift[j], state_rwkv_wkv[j], *wts)
            xp, xs = xp3.reshape(nb * seq, d), xs3.reshape(db * dseq, d)
            outs["rw_p"].append(st_p); outs["rsh_p"].append(sh_p)
            outs["rw_s"].append(st_s); outs["rsh_s"].append(sh_s)
        else:
            args = (norm_g[i], moba_w_in[j], moba_q_g[j], moba_k_g[j])
            qp, kp, vp, gp, _ = attn_project(xp, *args)
            qs, ks, vs, gs, _ = attn_project(xs, *args)
            op = prompt_attention(qp.reshape(nb, seq, d), kp.reshape(nb, seq, d),
                                  vp.reshape(nb, seq, d)).reshape(nb * seq, d)
            os_ = moba_decode_attention(qs, ks, vs, cache_moba_k, cache_moba_v, j, page_table)
            xp = out_project(xp, op, gp, moba_w_out[j])
            xs = out_project(xs, os_, gs, moba_w_out[j])
            outs["mk_p"].append(kp.reshape(heads_p)); outs["mv_p"].append(vp.reshape(heads_p))
            outs["mk_s"].append(ks.reshape(heads_s)); outs["mv_s"].append(vs.reshape(heads_s))
    stacked = {name: jnp.stack(v) for name, v in outs.items()}
    return (xp.reshape(nb, seq, d), xs.reshape(db, dseq, d),
            stacked["fk_p"], stacked["fv_p"], stacked["flf_p"], stacked["fk_s"], stacked["fv_s"], stacked["flf_s"],
            stacked["rw_p"], stacked["rsh_p"], stacked["rw_s"], stacked["rsh_s"],
            stacked["mk_p"], stacked["mv_p"], stacked["mk_s"], stacked["mv_s"])
```

```python
import functools

import numpy as np
import jax
import jax.numpy as jnp
from jax import lax
from jax.experimental import pallas as pl
from jax.experimental.pallas import tpu as pltpu

D_MODEL = 1024
N_HEADS = 16
HEAD_DIM = 64
LANES = 128
HEAD_PAIRS = N_HEADS // 2
PAGE_SIZE = 128
MOBA_BLOCK = 256
MOBA_TOPK = 3
RMS_EPS = 1e-6
GN_EPS = 64e-5
NEG_BIG = -1e30
VMEM_LIMIT = 56 * 1024 * 1024

f32 = jnp.float32
bf16 = jnp.bfloat16


def _cparams(*sem):
    return pltpu.CompilerParams(dimension_semantics=sem, vmem_limit_bytes=VMEM_LIMIT)


def _split3(x):
    p1 = x.astype(bf16)
    r1 = x - p1.astype(f32)
    p2 = r1.astype(bf16)
    p3 = (r1 - p2.astype(f32)).astype(bf16)
    return p1, p2, p3


def _dot(a, b):
    return jnp.dot(a, b, preferred_element_type=f32)


def _dot_nt(a, b, precision=None):
    return lax.dot_general(a, b, (((1,), (1,)), ((), ())), preferred_element_type=f32, precision=precision)


def _dot_exact_lhs(a_bf16, x):
    p1, p2, p3 = _split3(x)
    return _dot(a_bf16, p1) + _dot(a_bf16, p2) + _dot(a_bf16, p3)


def _log_sigmoid(z):
    return jnp.minimum(z, 0.0) - jnp.log1p(jnp.exp(-jnp.abs(z)))


def _first_head_lanes():
    return lax.broadcasted_iota(jnp.int32, (1, LANES), 1) < HEAD_DIM


def _per_head_sum(x):
    first = _first_head_lanes()
    s0 = jnp.sum(jnp.where(first, x, 0.0), axis=-1, keepdims=True)
    s1 = jnp.sum(jnp.where(first, 0.0, x), axis=-1, keepdims=True)
    return jnp.where(first, s0, s1)


def _attn_proj_kernel(x_ref, ng_ref, w_ref, wf_ref, bf_ref, qg_ref, kg_ref,
                      q_ref, k_ref, v_ref, g_ref, lf_ref):
    x = x_ref[...]
    ms = jnp.mean(x * x, axis=-1, keepdims=True)
    hn = (x * lax.rsqrt(ms + RMS_EPS) * ng_ref[...]).astype(bf16)
    d = D_MODEL
    for sec, (out_ref, gain_ref) in enumerate(((q_ref, qg_ref), (k_ref, kg_ref))):
        p = _dot(hn, w_ref[:, sec * d:(sec + 1) * d])
        for c in range(d // LANES):
            blk = p[:, c * LANES:(c + 1) * LANES]
            msq = _per_head_sum(blk * blk) * (1.0 / HEAD_DIM)
            out_ref[:, c * LANES:(c + 1) * LANES] = blk * lax.rsqrt(msq + RMS_EPS) * gain_ref[...]
    v_ref[...] = _dot(hn, w_ref[:, 2 * d:3 * d])
    g_ref[...] = _dot(hn, w_ref[:, 3 * d:4 * d])
    lf = _log_sigmoid(_dot(hn, wf_ref[...]) + bf_ref[...])
    lf_ref[...] = lf[:, :N_HEADS]


def _row_tile(n):
    for t in (512, 256, 128, 64, 32, 16, 8):
        if n % t == 0:
            return t
    raise ValueError(f"row count {n} is not a multiple of 8")


def attn_project(x2d, norm_g, w_in, q_g, k_g, b_f=None):
    n, d = x2d.shape
    tm = _row_tile(n)
    w_main = w_in[:, :4 * d].astype(bf16)
    if b_f is None:
        wf = jnp.zeros((d, LANES), bf16)
        bfp = jnp.zeros((1, LANES), f32)
    else:
        wf = jnp.pad(w_in[:, 4 * d:], ((0, 0), (0, LANES - N_HEADS))).astype(bf16)
        bfp = jnp.pad(b_f.reshape(1, N_HEADS), ((0, 0), (0, LANES - N_HEADS)))
    qg = jnp.tile(q_g.reshape(1, HEAD_DIM), (1, 2))
    kg = jnp.tile(k_g.reshape(1, HEAD_DIM), (1, 2))
    row = lambda i: (i, 0)
    const = lambda i: (0, 0)
    big = jax.ShapeDtypeStruct((n, d), f32)
    return pl.pallas_call(
        _attn_proj_kernel,
        grid=(n // tm,),
        in_specs=[pl.BlockSpec((tm, d), row), pl.BlockSpec((1, d), const),
                  pl.BlockSpec((d, 4 * d), const), pl.BlockSpec((d, LANES), const),
                  pl.BlockSpec((1, LANES), const), pl.BlockSpec((1, LANES), const),
                  pl.BlockSpec((1, LANES), const)],
        out_specs=[pl.BlockSpec((tm, d), row)] * 4 + [pl.BlockSpec((tm, N_HEADS), row)],
        out_shape=[big, big, big, big, jax.ShapeDtypeStruct((n, N_HEADS), f32)],
        compiler_params=_cparams("parallel"),
        name="attn_proj",
    )(x2d, norm_g.reshape(1, d), w_main, wf, bfp, qg, kg)


def _out_proj_kernel(x_ref, o_ref, g_ref, w_ref, y_ref):
    g = g_ref[...]
    a = o_ref[...] * (g * jax.nn.sigmoid(g))
    y_ref[...] = x_ref[...] + _dot(a.astype(bf16), w_ref[...])


def out_project(x2d, o2d, g2d, w_out):
    n, d = x2d.shape
    tm = _row_tile(n)
    row = lambda i: (i, 0)
    return pl.pallas_call(
        _out_proj_kernel,
        grid=(n // tm,),
        in_specs=[pl.BlockSpec((tm, d), row)] * 3 + [pl.BlockSpec((d, d), lambda i: (0, 0))],
        out_specs=pl.BlockSpec((tm, d), row),
        out_shape=jax.ShapeDtypeStruct((n, d), f32),
        compiler_params=_cparams("parallel"),
        name="out_proj",
    )(x2d, o2d, g2d, w_out.astype(bf16))


ATT_TILE = 256
Q_TILE = 512
PEN_LANE0 = 16


def _lane_ids():
    lane = lax.broadcasted_iota(jnp.int32, (1, LANES), 1)
    return lane, lane & (HEAD_DIM - 1)


def _fox_extras(lf_ref, cp_ref, seq):
    hp = pl.program_id(1)
    lane, lmod = _lane_ids()
    hrow = lax.broadcasted_iota(jnp.int32, (N_HEADS, LANES), 0)
    hlane = lax.broadcasted_iota(jnp.int32, (N_HEADS, LANES), 1)
    place = (((hrow == 2 * hp) & (hlane >= HEAD_DIM) & (hlane < HEAD_DIM + 6))
             | ((hrow == 2 * hp + 1) & (hlane < 6)))
    place = jnp.where(place, 1.0, 0.0).astype(bf16)
    l1, l2, l3 = _split3(lf_ref[0])
    lfp = _dot(l1, place) + _dot(l2, place) + _dot(l3, place)
    r = lax.broadcasted_iota(jnp.int32, (ATT_TILE, ATT_TILE), 0)
    c = lax.broadcasted_iota(jnp.int32, (ATT_TILE, ATT_TILE), 1)
    tri = jnp.where(r >= c, 1.0, 0.0).astype(bf16)
    carry = jnp.zeros((1, LANES), f32)
    for n in range(seq // ATT_TILE):
        cb = _dot_exact_lhs(tri, lfp[n * ATT_TILE:(n + 1) * ATT_TILE]) + carry
        carry = cb[ATT_TILE - 1:ATT_TILE, :]
        cp_ref[n * ATT_TILE:(n + 1) * ATT_TILE, :] = cb
    cum = cp_ref[...]
    p1 = cum.astype(bf16).astype(f32)
    r1 = cum - p1
    p2 = r1.astype(bf16).astype(f32)
    p3 = (r1 - p2).astype(bf16).astype(f32)
    third = jnp.where(lmod >= 3, lmod - 3, lmod)
    parts = jnp.where(third == 0, p1, jnp.where(third == 1, p2, p3))
    qx = jnp.where(lmod < 3, parts, jnp.where(lmod < 6, 1.0, 0.0))
    kx = jnp.where(lmod < 3, 1.0, jnp.where(lmod < 6, -parts, 0.0))
    return qx, kx


def _moba_extras(sp_ref, seq):
    lane, lmod = _lane_ids()
    sp = sp_ref[0, 0:1, :]
    pos = lax.broadcasted_iota(jnp.int32, (seq, LANES), 0)
    hi = (pos >> 7).astype(f32)
    lo = (pos & 127).astype(f32)
    blk = pos >> 8
    qx = jnp.where(lmod < 6, sp, jnp.where(lmod < 9, -hi, jnp.where(lmod < 12, -lo, 0.0)))
    onehot = jnp.where((lmod >= PEN_LANE0) & (lmod < PEN_LANE0 + 8) & (blk == lmod - PEN_LANE0), 1.0, 0.0)
    kx = jnp.where(lmod < 3, 128.0 * hi,
                   jnp.where(lmod < 6, lo,
                             jnp.where(lmod < 9, 128.0 * sp, jnp.where(lmod < 12, sp, onehot))))
    return qx, kx


def _moba_penalty(q_own, kmr_ref, own, lane0):
    lane = lax.broadcasted_iota(jnp.int32, (1, LANES), 1)
    gate = _dot_nt(q_own, kmr_ref[...], precision=lax.Precision.HIGHEST)
    cand = (lane >= lane0) & (lane < lane0 + own)
    g = jnp.where(cand, gate, -3e38)
    sel = jnp.zeros(gate.shape, jnp.bool_)
    for _ in range(MOBA_TOPK):
        mx = jnp.max(g, axis=-1, keepdims=True)
        idx = jnp.min(jnp.where(g == mx, lane, 4 * LANES), axis=-1, keepdims=True)
        pick = (lane == idx) & (mx > -3e38)
        sel = sel | pick
        g = jnp.where(pick, -3e38, g)
    return jnp.where(cand & jnp.logical_not(sel), NEG_BIG, 0.0)


def _flash_kernel(q_ref, k_ref, v_ref, aux_ref, o_ref, qx_ref, ka0_ref, ka1_ref, vb_ref, cp_ref,
                  kmr0_ref, kmr1_ref, *, mode, seq):
    first = _first_head_lanes()
    scale = HEAD_DIM ** -0.5
    if mode == "fox":
        qx, kx = _fox_extras(aux_ref, cp_ref, seq)
    else:
        qx, kx = _moba_extras(aux_ref, seq)
        nblk = seq // MOBA_BLOCK
        kmr0_ref[...] = jnp.zeros((LANES, LANES), f32)
        kmr1_ref[...] = jnp.zeros((LANES, LANES), f32)
        for n in range(nblk):
            km = jnp.sum(k_ref[0, n * MOBA_BLOCK:(n + 1) * MOBA_BLOCK, :], axis=0, keepdims=True) * (1.0 / MOBA_BLOCK)
            r0 = HEAD_DIM + PEN_LANE0 + n
            kmr0_ref[r0:r0 + 1, :] = km
            kmr1_ref[PEN_LANE0 + n:PEN_LANE0 + n + 1, :] = km
    qx_ref[...] = qx
    kk = k_ref[0]
    ka0_ref[...] = jnp.where(first, kk, kx).astype(bf16)
    ka1_ref[...] = jnp.where(first, kx, kk).astype(bf16)
    vb_ref[...] = v_ref[0].astype(bf16)

    tq = min(seq, Q_TILE)
    rr = lax.broadcasted_iota(jnp.int32, (tq, tq), 0)
    cc = lax.broadcasted_iota(jnp.int32, (tq, tq), 1)
    causal = cc <= rr
    for qi in range(seq // tq):
        row0 = qi * tq
        q_raw = q_ref[0, row0:row0 + tq, :]
        qxt = qx_ref[row0:row0 + tq, :]
        outs = []
        for hh in (0, 1):
            own_lanes = first if hh == 0 else jnp.logical_not(first)
            qa = jnp.where(own_lanes, q_raw * scale, qxt)
            if mode == "moba":
                kmr_ref = kmr0_ref if hh == 0 else kmr1_ref
                lane0 = (HEAD_DIM if hh == 0 else 0) + PEN_LANE0
                own = (row0 + lax.broadcasted_iota(jnp.int32, (tq, 1), 0)) // MOBA_BLOCK
                qa = qa + _moba_penalty(jnp.where(own_lanes, q_raw, 0.0), kmr_ref, own, lane0)
            qa = qa.astype(bf16)
            ka_ref = ka0_ref if hh == 0 else ka1_ref
            s_diag = jnp.where(causal, _dot_nt(qa, ka_ref[row0:row0 + tq, :]), NEG_BIG)
            m = jnp.max(s_diag, axis=-1, keepdims=True)
            if row0:
                s_past = _dot_nt(qa, ka_ref[0:row0, :])
                m = jnp.maximum(m, jnp.max(s_past, axis=-1, keepdims=True))
            p = jnp.exp(s_diag - m)
            l = jnp.sum(p, axis=-1, keepdims=True)
            acc = _dot(p.astype(bf16), vb_ref[row0:row0 + tq, :])
            if row0:
                p = jnp.exp(s_past - m)
                l = l + jnp.sum(p, axis=-1, keepdims=True)
                acc = acc + _dot(p.astype(bf16), vb_ref[0:row0, :])
            outs.append(acc / l)
        o_ref[0, row0:row0 + tq, :] = jnp.where(first, outs[0], outs[1])


def _alibi_piece_table():
    slopes = jnp.asarray(2.0 ** (-8.0 * np.arange(1, N_HEADS + 1, dtype=np.float32) / N_HEADS), f32)
    p1, p2, p3 = (p.astype(f32) for p in _split3(slopes))
    pieces = jnp.stack([p1, p2, p3], axis=-1)
    lane = np.arange(LANES)
    lmod = lane % HEAD_DIM
    owner = np.where(lane < HEAD_DIM, 1, 0)
    tab = pieces[2 * np.arange(HEAD_PAIRS)[:, None] + owner[None, :], (lmod % 3)[None, :]]
    tab = jnp.where((lmod < 12)[None, :], tab, 0.0)
    return jnp.broadcast_to(tab[:, None, :], (HEAD_PAIRS, 8, LANES))


def prompt_attention(q, k, v, logf=None):
    nb, seq, d = q.shape
    assert seq % ATT_TILE == 0 and seq % min(seq, Q_TILE) == 0 and seq // MOBA_BLOCK <= 8 and seq <= 128 * 256
    mode = "moba" if logf is None else "fox"
    qkv_spec = pl.BlockSpec((1, seq, LANES), lambda b, hp: (b, 0, hp))
    if mode == "fox":
        aux, aux_spec = logf, pl.BlockSpec((1, seq, N_HEADS), lambda b, hp: (b, 0, 0))
    else:
        aux, aux_spec = _alibi_piece_table(), pl.BlockSpec((1, 8, LANES), lambda b, hp: (hp, 0, 0))
    return pl.pallas_call(
        functools.partial(_flash_kernel, mode=mode, seq=seq),
        grid=(nb, HEAD_PAIRS),
        in_specs=[qkv_spec, qkv_spec, qkv_spec, aux_spec],
        out_specs=qkv_spec,
        out_shape=jax.ShapeDtypeStruct((nb, seq, d), f32),
        scratch_shapes=[pltpu.VMEM((seq, LANES), f32), pltpu.VMEM((seq, LANES), bf16),
                        pltpu.VMEM((seq, LANES), bf16), pltpu.VMEM((seq, LANES), bf16),
                        pltpu.VMEM((seq, LANES), f32), pltpu.VMEM((LANES, LANES), f32),
                        pltpu.VMEM((LANES, LANES), f32)],
        compiler_params=_cparams("parallel", "parallel"),
        name=f"{mode}_prompt_attn",
    )(q, k, v, aux)


def _rms_kernel(x_ref, ng_ref, h_ref):
    x = x_ref[...]
    ms = jnp.mean(x * x, axis=-1, keepdims=True)
    h_ref[...] = x * lax.rsqrt(ms + RMS_EPS) * ng_ref[...]


def rms_norm_rows(x2d, norm_g):
    n, d = x2d.shape
    tm = _row_tile(n)
    row = lambda i: (i, 0)
    return pl.pallas_call(
        _rms_kernel, grid=(n // tm,),
        in_specs=[pl.BlockSpec((tm, d), row), pl.BlockSpec((1, d), lambda i: (0, 0))],
        out_specs=pl.BlockSpec((tm, d), row),
        out_shape=jax.ShapeDtypeStruct((n, d), f32),
        compiler_params=_cparams("parallel"), name="rms_norm",
    )(x2d, norm_g.reshape(1, d))


def _rwkv_proj_kernel(h_ref, hp_ref, mu_ref, w_ref, w0_ref, w1_ref, w2_ref, a0_ref, a1_ref, a2_ref,
                      kk_ref, ka_ref, rk_ref,
                      r_ref, dec_ref, k_ref, v_ref, na_ref, b_ref, g_ref, bonus_ref):
    d = D_MODEL
    h = h_ref[...]
    delta = hp_ref[...] - h

    def mixed(i):
        return (h + delta * mu_ref[i:i + 1, :]).astype(bf16)

    r = _dot(mixed(0), w_ref[:, 0:d])
    k = _dot(mixed(1), w_ref[:, d:2 * d])
    v = _dot(mixed(2), w_ref[:, 2 * d:3 * d])
    g_ref[...] = _dot(mixed(3), w_ref[:, 3 * d:4 * d])
    ww = w0_ref[...] + _dot(jnp.tanh(_dot(mixed(4), w1_ref[...])).astype(bf16), w2_ref[...])
    dec_ref[...] = jnp.exp(-jnp.exp(_log_sigmoid(ww) - 0.5))
    aa = jax.nn.sigmoid(a0_ref[...] + _dot(_dot(mixed(5), a1_ref[...]).astype(bf16), a2_ref[...]))
    k2 = k * (1.0 + (aa - 1.0) * ka_ref[...])
    kkv = k * kk_ref[...]
    rkk = r * k2 * rk_ref[...]
    r_ref[...] = r
    k_ref[...] = k2
    v_ref[...] = v
    for c in range(d // LANES):
        sl = slice(c * LANES, (c + 1) * LANES)
        kc = kkv[:, sl]
        kn = kc / jnp.maximum(jnp.sqrt(_per_head_sum(kc * kc)), 1e-12)
        na_ref[:, sl] = -kn
        b_ref[:, sl] = kn * aa[:, sl]
        bonus_ref[:, sl] = _per_head_sum(rkk[:, sl]) * v[:, sl]


def rwkv_project(h2d, hprev2d, mu, w_in, w0, w1, w2, a0, a1, a2, k_k, k_a, r_k):
    n, d = h2d.shape
    tm = min(_row_tile(n), 256)
    row = lambda i: (i, 0)
    const = lambda i: (0, 0)
    lw, la = w1.shape[1], a1.shape[1]
    vec = lambda z: z.reshape(1, d)
    big = jax.ShapeDtypeStruct((n, d), f32)
    return pl.pallas_call(
        _rwkv_proj_kernel, grid=(n // tm,),
        in_specs=[pl.BlockSpec((tm, d), row), pl.BlockSpec((tm, d), row), pl.BlockSpec((6, d), const),
                  pl.BlockSpec((d, 4 * d), const), pl.BlockSpec((1, d), const),
                  pl.BlockSpec((d, lw), const), pl.BlockSpec((lw, d), const), pl.BlockSpec((1, d), const),
                  pl.BlockSpec((d, la), const), pl.BlockSpec((la, d), const),
                  pl.BlockSpec((1, d), const), pl.BlockSpec((1, d), const), pl.BlockSpec((1, d), const)],
        out_specs=[pl.BlockSpec((tm, d), row)] * 8,
        out_shape=[big] * 8,
        compiler_params=_cparams("parallel"), name="rwkv_proj",
    )(h2d, hprev2d, mu, w_in.astype(bf16), vec(w0), w1.astype(bf16), w2.astype(bf16), vec(a0),
      a1.astype(bf16), a2.astype(bf16), vec(k_k), vec(k_a), vec(r_k))


SCAN_CHUNK = 32


def _rwkv_scan_kernel(r_ref, w_ref, k_ref, v_ref, na_ref, b_ref, s0_ref, y_ref, st_ref, *, tc):
    @pl.when(pl.program_id(1) == 0)
    def _():
        st_ref[...] = s0_ref[...]

    zeros = jnp.zeros((HEAD_DIM, LANES), f32)

    def step(t, carry):
        def sa_pass(j, sa):
            return sa + st_ref[j] * na_ref[t, pl.ds(j, 1), :]

        sa = lax.fori_loop(0, HEAD_DIM, sa_pass, zeros, unroll=8)
        vt = v_ref[t]

        def update_pass(j, y):
            s_new = (st_ref[j] * w_ref[t, pl.ds(j, 1), :] + sa * b_ref[t, pl.ds(j, 1), :]
                     + vt * k_ref[t, pl.ds(j, 1), :])
            st_ref[j] = s_new
            return y + s_new * r_ref[t, pl.ds(j, 1), :]

        y_ref[t] = lax.fori_loop(0, HEAD_DIM, update_pass, zeros, unroll=8)
        return carry

    lax.fori_loop(0, tc, step, 0)


def rwkv_scan(r, w, k, v, na, b, s0):
    nb, t, d = r.shape
    lanes = nb * N_HEADS
    assert lanes % LANES == 0
    tc = min(t, SCAN_CHUNK)
    assert t % tc == 0

    def to_scan(z):
        return z.reshape(nb, t, N_HEADS, HEAD_DIM).transpose(1, 3, 0, 2).reshape(t, HEAD_DIM, lanes)

    s0_l = s0.transpose(3, 2, 0, 1).reshape(HEAD_DIM, HEAD_DIM, lanes)
    seq_spec = pl.BlockSpec((tc, HEAD_DIM, LANES), lambda g, i: (i, 0, g))
    st_spec = pl.BlockSpec((HEAD_DIM, HEAD_DIM, LANES), lambda g, i: (0, 0, g))
    y, st = pl.pallas_call(
        functools.partial(_rwkv_scan_kernel, tc=tc),
        grid=(lanes // LANES, t // tc),
        in_specs=[seq_spec] * 6 + [st_spec],
        out_specs=[seq_spec, st_spec],
        out_shape=[jax.ShapeDtypeStruct((t, HEAD_DIM, lanes), f32),
                   jax.ShapeDtypeStruct((HEAD_DIM, HEAD_DIM, lanes), f32)],
        compiler_params=_cparams("parallel", "arbitrary"), name="rwkv_scan",
    )(*(to_scan(z) for z in (r, w, k, v, na, b)), s0_l)
    y = y.reshape(t, HEAD_DIM, nb, N_HEADS).transpose(2, 0, 3, 1).reshape(nb, t, d)
    st = st.reshape(HEAD_DIM, HEAD_DIM, nb, N_HEADS).transpose(2, 3, 1, 0)
    return y, st


def _rwkv_post_kernel(x_ref, y_ref, bonus_ref, g_ref, lng_ref, lnb_ref, w_ref, o_ref, a_ref):
    for c in range(D_MODEL // LANES):
        sl = slice(c * LANES, (c + 1) * LANES)
        y = y_ref[:, sl]
        dev = y - _per_head_sum(y) * (1.0 / HEAD_DIM)
        var = _per_head_sum(dev * dev) * (1.0 / HEAD_DIM)
        yn = dev * lax.rsqrt(var + GN_EPS) * lng_ref[:, sl] + lnb_ref[:, sl] + bonus_ref[:, sl]
        g = g_ref[:, sl]
        a_ref[:, sl] = (yn * (g * jax.nn.sigmoid(g))).astype(bf16)
    o_ref[...] = x_ref[...] + _dot(a_ref[...], w_ref[...])


def rwkv_post(x2d, y2d, bonus2d, g2d, ln_g, ln_b, w_out):
    n, d = x2d.shape
    tm = _row_tile(n)
    row = lambda i: (i, 0)
    const = lambda i: (0, 0)
    return pl.pallas_call(
        _rwkv_post_kernel, grid=(n // tm,),
        in_specs=[pl.BlockSpec((tm, d), row)] * 4 + [pl.BlockSpec((1, d), const)] * 2 + [pl.BlockSpec((d, d), const)],
        out_specs=pl.BlockSpec((tm, d), row),
        out_shape=jax.ShapeDtypeStruct((n, d), f32),
        scratch_shapes=[pltpu.VMEM((tm, d), bf16)],
        compiler_params=_cparams("parallel"), name="rwkv_post",
    )(x2d, y2d, bonus2d, g2d, ln_g.reshape(1, d), ln_b.reshape(1, d), w_out.astype(bf16))


def rwkv_layer(x, x_last, s0, norm_g, mu, w_in, w0, w1, w2, a0, a1, a2, k_k, k_a, r_k, ln_g, ln_b, w_out):
    nb, t, d = x.shape
    x2d = x.reshape(nb * t, d)
    hn = rms_norm_rows(x2d, norm_g).reshape(nb, t, d)
    hprev = jnp.concatenate([x_last[:, None, :], hn[:, :-1]], axis=1)
    r, dec, k, v, na, b, g, bonus = rwkv_project(hn.reshape(nb * t, d), hprev.reshape(nb * t, d), mu, w_in,
                                                 w0, w1, w2, a0, a1, a2, k_k, k_a, r_k.reshape(-1))
    shp = (nb, t, d)
    y, st = rwkv_scan(*(z.reshape(shp) for z in (r, dec, k, v, na, b)), s0)
    out = rwkv_post(x2d, y.reshape(nb * t, d), bonus, g, ln_g, ln_b, w_out)
    return out.reshape(nb, t, d), st, hn[:, -1]


PAGES_PER_STEP = 4


def _positions_last(pool):
    return jnp.moveaxis(pool, 2, -1)


def _page_specs(pool, layer, n_arrays=PAGES_PER_STEP):
    blk = (1, 1) + pool.shape[2:]
    zeros = (0,) * (pool.ndim - 2)
    return [pl.BlockSpec(blk, functools.partial(
        lambda b, jg, pt, p: (layer, pt[b, jg * PAGES_PER_STEP + p]) + zeros, p=p)) for p in range(n_arrays)]


def _page_scores(q_ref, k_ref):
    return jnp.concatenate([_dot(q_ref[0, h], k_ref[0, 0, h]) for h in range(N_HEADS)], axis=0)


def _page_values(p, v_ref, t):
    return jnp.concatenate([_dot_nt(p[h * t:(h + 1) * t, :], v_ref[0, 0, h]) for h in range(N_HEADS)], axis=0)


def _softmax_pages(scores, v_refs, t, m_old=None):
    mx = scores[0]
    for s in scores[1:]:
        mx = jnp.maximum(mx, s)
    m_new = jnp.broadcast_to(jnp.max(mx, axis=-1, keepdims=True), mx.shape)
    if m_old is not None:
        m_new = jnp.maximum(m_new, m_old)
    probs = [jnp.exp(s - m_new) for s in scores]
    psum = probs[0]
    for p in probs[1:]:
        psum = psum + p
    l_new = jnp.broadcast_to(jnp.sum(psum, axis=-1, keepdims=True), psum.shape)
    acc = _page_values(probs[0], v_refs[0], t)
    for p, v_ref in zip(probs[1:], v_refs[1:]):
        acc = acc + _page_values(p, v_ref, t)
    return m_new, l_new, acc


def _head_rows_selector(t):
    r = lax.broadcasted_iota(jnp.int32, (N_HEADS * t, N_HEADS), 0)
    c = lax.broadcasted_iota(jnp.int32, (N_HEADS * t, N_HEADS), 1)
    return jnp.where((r >= c * t) & (r < (c + 1) * t), 1.0, 0.0).astype(bf16)


def _fox_decode_kernel(pt_ref, q_ref, *refs):
    k_refs = refs[0:PAGES_PER_STEP]
    v_refs = refs[PAGES_PER_STEP:2 * PAGES_PER_STEP]
    lf_refs = refs[2 * PAGES_PER_STEP:3 * PAGES_PER_STEP]
    m_ref, l_ref, acc_ref, tot_ref = refs[3 * PAGES_PER_STEP:]
    t = q_ref.shape[2]

    @pl.when(pl.program_id(1) == 0)
    def _():
        m_ref[...] = jnp.full(m_ref.shape, NEG_BIG, f32)
        l_ref[...] = jnp.zeros(l_ref.shape, f32)
        acc_ref[...] = jnp.zeros(acc_ref.shape, f32)
        tot_ref[...] = jnp.zeros(tot_ref.shape, f32)

    scale = HEAD_DIM ** -0.5
    r = lax.broadcasted_iota(jnp.int32, (PAGE_SIZE, PAGE_SIZE), 0)
    c = lax.broadcasted_iota(jnp.int32, (PAGE_SIZE, PAGE_SIZE), 1)
    upper = jnp.where(r <= c, 1.0, 0.0).astype(bf16)
    repeat_rows = _head_rows_selector(t)
    tot = tot_ref[0]
    scores = []
    for p in range(PAGES_PER_STEP):
        l1, l2, l3 = _split3(lf_refs[p][0, 0])
        local = _dot(l1, upper) + _dot(l2, upper) + _dot(l3, upper)
        cum = _dot_exact_lhs(repeat_rows, tot + local)
        tot = tot + jnp.broadcast_to(local[:, PAGE_SIZE - 1:PAGE_SIZE], local.shape)
        scores.append(_page_scores(q_ref, k_refs[p]) * scale - cum)
    tot_ref[0] = tot
    m_old = m_ref[0]
    m_new, l_new, acc = _softmax_pages(scores, v_refs, t, m_old)
    alpha = jnp.exp(m_old - m_new)
    m_ref[0] = m_new
    l_ref[0] = alpha * l_ref[0] + l_new
    acc_ref[0] = alpha[:, :HEAD_DIM] * acc_ref[0] + acc


def _fox_finish_kernel(q_ref, k_ref, v_ref, lf_ref, lft_ref, m_ref, l_ref, acc_ref, tot_ref, o_ref):
    t = q_ref.shape[2]
    scale = HEAD_DIM ** -0.5
    r = lax.broadcasted_iota(jnp.int32, (t, t), 0)
    c = lax.broadcasted_iota(jnp.int32, (t, t), 1)
    lower = jnp.where(r >= c, 1.0, 0.0).astype(bf16)
    upper = jnp.where(r <= c, 1.0, 0.0).astype(bf16)
    cn_col = _dot_exact_lhs(lower, lf_ref[0])
    l1, l2, l3 = _split3(lft_ref[0])
    cn_row = _dot(l1, upper) + _dot(l2, upper) + _dot(l3, upper)
    for h in range(N_HEADS):
        rows = slice(h * t, (h + 1) * t)
        cq = cn_col[:, h:h + 1]
        s = _dot_nt(q_ref[0, h], k_ref[0, h]) * scale + cq - cn_row[h:h + 1, :]
        s = jnp.where(c <= r, s, NEG_BIG)
        m_past = m_ref[0, rows, 0:1] + cq + tot_ref[0, h:h + 1, 0:1]
        m_tot = jnp.maximum(m_past, jnp.max(s, axis=-1, keepdims=True))
        w_past = jnp.exp(m_past - m_tot)
        p = jnp.exp(s - m_tot)
        l_tot = w_past * l_ref[0, rows, 0:1] + jnp.sum(p, axis=-1, keepdims=True)
        o_ref[0, h] = (w_past * acc_ref[0, rows, :] + _dot(p, v_ref[0, h])) / l_tot


def _heads_first(z2d, nb, t):
    return z2d.reshape(nb, t, N_HEADS, HEAD_DIM).transpose(0, 2, 1, 3)


def _heads_last(z4d):
    nb, _, t, _ = z4d.shape
    return z4d.transpose(0, 2, 1, 3).reshape(nb * t, D_MODEL)


def fox_decode_attention(q2d, k2d, v2d, logf2d, pool_k, pool_v, pool_lf, layer, page_table):
    nb, n_pages = page_table.shape
    t = q2d.shape[0] // nb
    assert n_pages % PAGES_PER_STEP == 0
    qh, kh, vh = (_heads_first(z, nb, t) for z in (q2d, k2d, v2d))
    lf = logf2d.reshape(nb, t, N_HEADS)
    pool_k, pool_v, pool_lft = (_positions_last(z) for z in (pool_k, pool_v, pool_lf))
    rows = N_HEADS * t
    head_spec = pl.BlockSpec((1, N_HEADS, t, HEAD_DIM), lambda b, jg, pt: (b, 0, 0, 0))
    stat_spec = pl.BlockSpec((1, rows, PAGE_SIZE), lambda b, jg, pt: (b, 0, 0))
    acc_spec = pl.BlockSpec((1, rows, HEAD_DIM), lambda b, jg, pt: (b, 0, 0))
    tot_spec = pl.BlockSpec((1, N_HEADS, PAGE_SIZE), lambda b, jg, pt: (b, 0, 0))
    stat = jax.ShapeDtypeStruct((nb, rows, PAGE_SIZE), f32)
    m, l, acc, tot = pl.pallas_call(
        _fox_decode_kernel,
        grid_spec=pltpu.PrefetchScalarGridSpec(
            num_scalar_prefetch=1, grid=(nb, n_pages // PAGES_PER_STEP),
            in_specs=[head_spec] + _page_specs(pool_k, layer) + _page_specs(pool_v, layer) + _page_specs(pool_lft, layer),
            out_specs=[stat_spec, stat_spec, acc_spec, tot_spec]),
        out_shape=[stat, stat, jax.ShapeDtypeStruct((nb, rows, HEAD_DIM), f32),
                   jax.ShapeDtypeStruct((nb, N_HEADS, PAGE_SIZE), f32)],
        compiler_params=_cparams("parallel", "arbitrary"), name="fox_decode",
    )(page_table, qh, *([pool_k] * PAGES_PER_STEP), *([pool_v] * PAGES_PER_STEP), *([pool_lft] * PAGES_PER_STEP))
    hs = pl.BlockSpec((1, N_HEADS, t, HEAD_DIM), lambda b: (b, 0, 0, 0))
    ss = pl.BlockSpec((1, rows, PAGE_SIZE), lambda b: (b, 0, 0))
    o = pl.pallas_call(
        _fox_finish_kernel, grid=(nb,),
        in_specs=[hs, hs, hs, pl.BlockSpec((1, t, N_HEADS), lambda b: (b, 0, 0)),
                  pl.BlockSpec((1, N_HEADS, t), lambda b: (b, 0, 0)), ss, ss,
                  pl.BlockSpec((1, rows, HEAD_DIM), lambda b: (b, 0, 0)),
                  pl.BlockSpec((1, N_HEADS, PAGE_SIZE), lambda b: (b, 0, 0))],
        out_specs=hs, out_shape=jax.ShapeDtypeStruct((nb, N_HEADS, t, HEAD_DIM), f32),
        compiler_params=_cparams("parallel"), name="fox_decode_finish",
    )(qh, kh, vh, lf, jnp.swapaxes(lf, 1, 2), m, l, acc, tot)
    return _heads_last(o)


def _alibi_slopes_np():
    return 2.0 ** (-8.0 * np.arange(1, N_HEADS + 1, dtype=np.float32) / N_HEADS)


def _moba_decode_kernel(pt_ref, q_ref, qt_ref, *refs, past_len):
    k_refs = refs[0:PAGES_PER_STEP]
    v_refs = refs[PAGES_PER_STEP:2 * PAGES_PER_STEP]
    slope_ref, m_ref, l_ref, acc_ref, gate_ref = refs[2 * PAGES_PER_STEP:]
    t = q_ref.shape[2]
    rows = N_HEADS * t
    scale = HEAD_DIM ** -0.5
    pages_per_block = MOBA_BLOCK // PAGE_SIZE
    page0 = pl.program_id(1) * PAGES_PER_STEP
    qpos = past_len + lax.broadcasted_iota(jnp.int32, (rows, PAGE_SIZE), 0) % t
    lane = lax.broadcasted_iota(jnp.int32, (rows, PAGE_SIZE), 1)
    for blk in range(PAGES_PER_STEP // pages_per_block):
        pages = range(blk * pages_per_block, (blk + 1) * pages_per_block)
        kblock = k_refs[pages[0]][0, 0]
        for p in pages[1:]:
            kblock = kblock + k_refs[p][0, 0]
        kmean = jnp.sum(kblock, axis=-1, keepdims=True) * (1.0 / MOBA_BLOCK)
        for h in range(N_HEADS):
            gate_ref[0, blk, h:h + 1, :] = jnp.sum(qt_ref[0, h] * kmean[h], axis=0, keepdims=True)
        scores = []
        for p in pages:
            dist = (qpos - ((page0 + p) * PAGE_SIZE + lane)).astype(f32)
            scores.append(_page_scores(q_ref, k_refs[p]) * scale - slope_ref[...] * dist)
        m, l, acc = _softmax_pages(scores, [v_refs[p] for p in pages], t)
        m_ref[0, blk] = m[:, :HEAD_DIM]
        l_ref[0, blk] = l[:, :HEAD_DIM]
        acc_ref[0, blk] = acc


def _moba_finish_kernel(q_ref, k_ref, v_ref, m_ref, l_ref, acc_ref, gate_ref, o_ref):
    t = q_ref.shape[2]
    n_blocks = m_ref.shape[1]
    scale = HEAD_DIM ** -0.5
    slopes = _alibi_slopes_np()
    r = lax.broadcasted_iota(jnp.int32, (t, t), 0)
    c = lax.broadcasted_iota(jnp.int32, (t, t), 1)
    lane = lax.broadcasted_iota(jnp.int32, (1, n_blocks), 1)
    g = gate_ref[0]
    sel = jnp.zeros(g.shape, f32)
    for _ in range(min(MOBA_TOPK, n_blocks)):
        mx = jnp.max(g, axis=-1, keepdims=True)
        idx = jnp.min(jnp.where(g == mx, lane, 4 * n_blocks), axis=-1, keepdims=True)
        pick = lane == idx
        sel = jnp.where(pick, 1.0, sel)
        g = jnp.where(pick, -3e38, g)
    for h in range(N_HEADS):
        qh = q_ref[0, h]
        s = _dot_nt(qh, k_ref[0, h]) * scale - float(slopes[h]) * (r - c).astype(f32)
        s = jnp.where(c <= r, s, NEG_BIG)
        m_own = jnp.max(s, axis=-1, keepdims=True)
        p = jnp.exp(s - m_own)
        l_own = jnp.sum(p, axis=-1, keepdims=True)
        acc_own = _dot(p, v_ref[0, h])
        rows = slice(h * t, (h + 1) * t)
        sel_h = sel[rows, :]
        chosen = [jnp.broadcast_to(sel_h[:, n:n + 1], (t, HEAD_DIM)) for n in range(n_blocks)]
        m_blk = [jnp.where(chosen[n] > 0.0, m_ref[0, n, rows, :], NEG_BIG) for n in range(n_blocks)]
        m_tot = jnp.broadcast_to(m_own, (t, HEAD_DIM))
        for n in range(n_blocks):
            m_tot = jnp.maximum(m_tot, m_blk[n])
        w_own = jnp.exp(m_own - m_tot)
        l_tot = w_own * l_own
        acc_tot = w_own * acc_own
        for n in range(n_blocks):
            w = jnp.exp(m_blk[n] - m_tot) * chosen[n]
            l_tot = l_tot + w * l_ref[0, n, rows, :]
            acc_tot = acc_tot + w * acc_ref[0, n, rows, :]
        o_ref[0, h] = acc_tot / l_tot


def moba_decode_attention(q2d, k2d, v2d, pool_k, pool_v, layer, page_table):
    nb, n_pages = page_table.shape
    t = q2d.shape[0] // nb
    past_len = n_pages * PAGE_SIZE
    assert past_len % MOBA_BLOCK == 0 and t <= MOBA_BLOCK and n_pages % PAGES_PER_STEP == 0
    n_blocks = past_len // MOBA_BLOCK
    bps = PAGES_PER_STEP * PAGE_SIZE // MOBA_BLOCK
    qh, kh, vh = (_heads_first(z, nb, t) for z in (q2d, k2d, v2d))
    qt = jnp.swapaxes(qh, 2, 3)
    pool_k, pool_v = _positions_last(pool_k), _positions_last(pool_v)
    head_spec = pl.BlockSpec((1, N_HEADS, t, HEAD_DIM), lambda b, jg, pt: (b, 0, 0, 0))
    qt_spec = pl.BlockSpec((1, N_HEADS, HEAD_DIM, t), lambda b, jg, pt: (b, 0, 0, 0))
    rows = N_HEADS * t
    slope_rows = jnp.broadcast_to(jnp.repeat(jnp.asarray(_alibi_slopes_np()), t)[:, None], (rows, PAGE_SIZE))
    part_spec = pl.BlockSpec((1, bps, rows, HEAD_DIM), lambda b, jg, pt: (b, jg, 0, 0))
    part = jax.ShapeDtypeStruct((nb, n_blocks, rows, HEAD_DIM), f32)
    m, l, acc, gate = pl.pallas_call(
        functools.partial(_moba_decode_kernel, past_len=past_len),
        grid_spec=pltpu.PrefetchScalarGridSpec(
            num_scalar_prefetch=1, grid=(nb, n_pages // PAGES_PER_STEP),
            in_specs=([head_spec, qt_spec] + _page_specs(pool_k, layer) + _page_specs(pool_v, layer)
                      + [pl.BlockSpec((rows, PAGE_SIZE), lambda b, jg, pt: (0, 0))]),
            out_specs=[part_spec, part_spec, part_spec,
                       pl.BlockSpec((1, bps, N_HEADS, t), lambda b, jg, pt: (b, jg, 0, 0))]),
        out_shape=[part, part, part, jax.ShapeDtypeStruct((nb, n_blocks, N_HEADS, t), f32)],
        compiler_params=_cparams("parallel", "parallel"), name="moba_decode",
    )(page_table, qh, qt, *([pool_k] * PAGES_PER_STEP), *([pool_v] * PAGES_PER_STEP), slope_rows)
    gate = gate.transpose(0, 2, 3, 1).reshape(nb, rows, n_blocks)
    hs = pl.BlockSpec((1, N_HEADS, t, HEAD_DIM), lambda b: (b, 0, 0, 0))
    ps = pl.BlockSpec((1, n_blocks, rows, HEAD_DIM), lambda b: (b, 0, 0, 0))
    o = pl.pallas_call(
        _moba_finish_kernel, grid=(nb,),
        in_specs=[hs, hs, hs, ps, ps, ps, pl.BlockSpec((1, N_HEADS * t, n_blocks), lambda b: (b, 0, 0))],
        out_specs=hs, out_shape=jax.ShapeDtypeStruct((nb, N_HEADS, t, HEAD_DIM), f32),
        compiler_params=_cparams("parallel"), name="moba_decode_finish",
    )(qh, kh, vh, m, l, acc, gate)
    return _heads_last(o)


N_MIXERS = 3


def kernel(x_prompt, x_sample, cache_fox_k, cache_fox_v, cache_fox_logf, state_rwkv_wkv, state_rwkv_shift, cache_moba_k, cache_moba_v, page_table, norm_g, fox_w_in, fox_b_f, fox_q_g, fox_k_g, fox_w_out, rwkv_mu, rwkv_w_in, rwkv_w0, rwkv_w1, rwkv_w2, rwkv_a0, rwkv_a1, rwkv_a2, rwkv_k_k, rwkv_k_a, rwkv_r_k, rwkv_ln_g, rwkv_ln_b, rwkv_w_out, moba_w_in, moba_q_g, moba_k_g, moba_w_out):
    nb, seq, d = x_prompt.shape
    db, dseq, _ = x_sample.shape
    depth = norm_g.shape[0]
    xp = x_prompt.reshape(nb * seq, d)
    xs = x_sample.reshape(db * dseq, d)
    heads_p = (nb, seq, N_HEADS, HEAD_DIM)
    heads_s = (db, dseq, N_HEADS, HEAD_DIM)
    outs = {name: [] for name in ("fk_p", "fv_p", "flf_p", "fk_s", "fv_s", "flf_s", "rw_p", "rsh_p", "rw_s", "rsh_s",
                                  "mk_p", "mv_p", "mk_s", "mv_s")}
    for i in range(depth):
        kind, j = i % N_MIXERS, i // N_MIXERS
        if kind == 0:
            args = (norm_g[i], fox_w_in[j], fox_q_g[j], fox_k_g[j], fox_b_f[j])
            qp, kp, vp, gp, lfp = attn_project(xp, *args)
            qs, ks, vs, gs, lfs = attn_project(xs, *args)
            op = prompt_attention(qp.reshape(nb, seq, d), kp.reshape(nb, seq, d), vp.reshape(nb, seq, d),
                                  lfp.reshape(nb, seq, N_HEADS)).reshape(nb * seq, d)
            os_ = fox_decode_attention(qs, ks, vs, lfs, cache_fox_k, cache_fox_v, cache_fox_logf, j, page_table)
            xp = out_project(xp, op, gp, fox_w_out[j])
            xs = out_project(xs, os_, gs, fox_w_out[j])
            outs["fk_p"].append(kp.reshape(heads_p)); outs["fv_p"].append(vp.reshape(heads_p))
            outs["flf_p"].append(lfp.reshape(nb, seq, N_HEADS))
            outs["fk_s"].append(ks.reshape(heads_s)); outs["fv_s"].append(vs.reshape(heads_s))
            outs["flf_s"].append(lfs.reshape(db, dseq, N_HEADS))
        elif kind == 1:
            wts = (norm_g[i], rwkv_mu[j], rwkv_w_in[j], rwkv_w0[j], rwkv_w1[j], rwkv_w2[j], rwkv_a0[j], rwkv_a1[j],
                   rwkv_a2[j], rwkv_k_k[j], rwkv_k_a[j], rwkv_r_k[j], rwkv_ln_g[j], rwkv_ln_b[j], rwkv_w_out[j])
            xp3, st_p, sh_p = rwkv_layer(xp.reshape(nb, seq, d), jnp.zeros((nb, d), f32),
                                         jnp.zeros((nb, N_HEADS, HEAD_DIM, HEAD_DIM), f32), *wts)
            xs3, st_s, sh_s = rwkv_layer(xs.reshape(db, dseq, d), state_rwkv_shift[j], state_rwkv_wkv[j], *wts)
            xp, xs = xp3.reshape(nb * seq, d), xs3.reshape(db * dseq, d)
            outs["rw_p"].append(st_p); outs["rsh_p"].append(sh_p)
            outs["rw_s"].append(st_s); outs["rsh_s"].append(sh_s)
        else:
            args = (norm_g[i], moba_w_in[j], moba_q_g[j], moba_k_g[j])
            qp, kp, vp, gp, _ = attn_project(xp, *args)
            qs, ks, vs, gs, _ = attn_project(xs, *args)
            op = prompt_attention(qp.reshape(nb, seq, d), kp.reshape(nb, seq, d),
                                  vp.reshape(nb, seq, d)).reshape(nb * seq, d)
            os_ = moba_decode_attention(qs, ks, vs, cache_moba_k, cache_moba_v, j, page_table)
            xp = out_project(xp, op, gp, moba_w_out[j])
            xs = out_project(xs, os_, gs, moba_w_out[j])
            outs["mk_p"].append(kp.reshape(heads_p)); outs["mv_p"].append(vp.reshape(heads_p))
            outs["mk_s"].append(ks.reshape(heads_s)); outs["mv_s"].append(vs.reshape(heads_s))
    stacked = {name: jnp.stack(v) for name, v in outs.items()}
    return (xp.reshape(nb, seq, d), xs.reshape(db, dseq, d),
            stacked["fk_p"], stacked["fv_p"], stacked["flf_p"], stacked["fk_s"], stacked["fv_s"], stacked["flf_s"],
            stacked["rw_p"], stacked["rsh_p"], stacked["rw_s"], stacked["rsh_s"],
            stacked["mk_p"], stacked["mv_p"], stacked["mk_s"], stacked["mv_s"])
```

```python
import functools

import numpy as np
import jax
import jax.numpy as jnp
from jax import lax
from jax.experimental import pallas as pl
from jax.experimental.pallas import tpu as pltpu

D_MODEL = 1024
N_HEADS = 16
HEAD_DIM = 64
LANES = 128
HEAD_PAIRS = N_HEADS // 2
PAGE_SIZE = 128
MOBA_BLOCK = 256
MOBA_TOPK = 3
RMS_EPS = 1e-6
GN_EPS = 64e-5
NEG_BIG = -1e30
VMEM_LIMIT = 56 * 1024 * 1024

f32 = jnp.float32
bf16 = jnp.bfloat16


def _cparams(*sem):
    return pltpu.CompilerParams(dimension_semantics=sem, vmem_limit_bytes=VMEM_LIMIT)


def _split3(x):
    p1 = x.astype(bf16)
    r1 = x - p1.astype(f32)
    p2 = r1.astype(bf16)
    p3 = (r1 - p2.astype(f32)).astype(bf16)
    return p1, p2, p3


def _dot(a, b):
    return jnp.dot(a, b, preferred_element_type=f32)


def _dot_nt(a, b, precision=None):
    return lax.dot_general(a, b, (((1,), (1,)), ((), ())), preferred_element_type=f32, precision=precision)


def _dot_exact_lhs(a_bf16, x):
    p1, p2, p3 = _split3(x)
    return _dot(a_bf16, p1) + _dot(a_bf16, p2) + _dot(a_bf16, p3)


def _log_sigmoid(z):
    return jnp.minimum(z, 0.0) - jnp.log1p(jnp.exp(-jnp.abs(z)))


def _first_head_lanes():
    return lax.broadcasted_iota(jnp.int32, (1, LANES), 1) < HEAD_DIM


def _per_head_sum(x):
    first = _first_head_lanes()
    s0 = jnp.sum(jnp.where(first, x, 0.0), axis=-1, keepdims=True)
    s1 = jnp.sum(jnp.where(first, 0.0, x), axis=-1, keepdims=True)
    return jnp.where(first, s0, s1)


def _attn_proj_kernel(x_ref, ng_ref, w_ref, wf_ref, bf_ref, qg_ref, kg_ref,
                      q_ref, k_ref, v_ref, g_ref, lf_ref):
    x = x_ref[...]
    ms = jnp.mean(x * x, axis=-1, keepdims=True)
    hn = (x * lax.rsqrt(ms + RMS_EPS) * ng_ref[...]).astype(bf16)
    d = D_MODEL
    for sec, (out_ref, gain_ref) in enumerate(((q_ref, qg_ref), (k_ref, kg_ref))):
        p = _dot(hn, w_ref[:, sec * d:(sec + 1) * d])
        for c in range(d // LANES):
            blk = p[:, c * LANES:(c + 1) * LANES]
            msq = _per_head_sum(blk * blk) * (1.0 / HEAD_DIM)
            out_ref[:, c * LANES:(c + 1) * LANES] = blk * lax.rsqrt(msq + RMS_EPS) * gain_ref[...]
    v_ref[...] = _dot(hn, w_ref[:, 2 * d:3 * d])
    g_ref[...] = _dot(hn, w_ref[:, 3 * d:4 * d])
    lf = _log_sigmoid(_dot(hn, wf_ref[...]) + bf_ref[...])
    lf_ref[...] = lf[:, :N_HEADS]


def _row_tile(n):
    for t in (512, 256, 128, 64, 32, 16, 8):
        if n % t == 0:
            return t
    raise ValueError(f"row count {n} is not a multiple of 8")


def attn_project(x2d, norm_g, w_in, q_g, k_g, b_f=None):
    n, d = x2d.shape
    tm = _row_tile(n)
    w_main = w_in[:, :4 * d].astype(bf16)
    if b_f is None:
        wf = jnp.zeros((d, LANES), bf16)
        bfp = jnp.zeros((1, LANES), f32)
    else:
        wf = jnp.pad(w_in[:, 4 * d:], ((0, 0), (0, LANES - N_HEADS))).astype(bf16)
        bfp = jnp.pad(b_f.reshape(1, N_HEADS), ((0, 0), (0, LANES - N_HEADS)))
    qg = jnp.tile(q_g.reshape(1, HEAD_DIM), (1, 2))
    kg = jnp.tile(k_g.reshape(1, HEAD_DIM), (1, 2))
    row = lambda i: (i, 0)
    const = lambda i: (0, 0)
    big = jax.ShapeDtypeStruct((n, d), f32)
    return pl.pallas_call(
        _attn_proj_kernel,
        grid=(n // tm,),
        in_specs=[pl.BlockSpec((tm, d), row), pl.BlockSpec((1, d), const),
                  pl.BlockSpec((d, 4 * d), const), pl.BlockSpec((d, LANES), const),
                  pl.BlockSpec((1, LANES), const), pl.BlockSpec((1, LANES), const),
                  pl.BlockSpec((1, LANES), const)],
        out_specs=[pl.BlockSpec((tm, d), row)] * 4 + [pl.BlockSpec((tm, N_HEADS), row)],
        out_shape=[big, big, big, big, jax.ShapeDtypeStruct((n, N_HEADS), f32)],
        compiler_params=_cparams("parallel"),
        name="attn_proj",
    )(x2d, norm_g.reshape(1, d), w_main, wf, bfp, qg, kg)


def _out_proj_kernel(x_ref, o_ref, g_ref, w_ref, y_ref):
    g = g_ref[...]
    a = o_ref[...] * (g * jax.nn.sigmoid(g))
    y_ref[...] = x_ref[...] + _dot(a.astype(bf16), w_ref[...])


def out_project(x2d, o2d, g2d, w_out):
    n, d = x2d.shape
    tm = _row_tile(n)
    row = lambda i: (i, 0)
    return pl.pallas_call(
        _out_proj_kernel,
        grid=(n // tm,),
        in_specs=[pl.BlockSpec((tm, d), row)] * 3 + [pl.BlockSpec((d, d), lambda i: (0, 0))],
        out_specs=pl.BlockSpec((tm, d), row),
        out_shape=jax.ShapeDtypeStruct((n, d), f32),
        compiler_params=_cparams("parallel"),
        name="out_proj",
    )(x2d, o2d, g2d, w_out.astype(bf16))


ATT_TILE = 256
Q_TILE = 512
PEN_LANE0 = 16


def _lane_ids():
    lane = lax.broadcasted_iota(jnp.int32, (1, LANES), 1)
    return lane, lane & (HEAD_DIM - 1)


def _fox_extras(lf_ref, cp_ref, seq):
    hp = pl.program_id(1)
    lane, lmod = _lane_ids()
    hrow = lax.broadcasted_iota(jnp.int32, (N_HEADS, LANES), 0)
    hlane = lax.broadcasted_iota(jnp.int32, (N_HEADS, LANES), 1)
    place = (((hrow == 2 * hp) & (hlane >= HEAD_DIM) & (hlane < HEAD_DIM + 6))
             | ((hrow == 2 * hp + 1) & (hlane < 6)))
    place = jnp.where(place, 1.0, 0.0).astype(bf16)
    l1, l2, l3 = _split3(lf_ref[0])
    lfp = _dot(l1, place) + _dot(l2, place) + _dot(l3, place)
    r = lax.broadcasted_iota(jnp.int32, (ATT_TILE, ATT_TILE), 0)
    c = lax.broadcasted_iota(jnp.int32, (ATT_TILE, ATT_TILE), 1)
    tri = jnp.where(r >= c, 1.0, 0.0).astype(bf16)
    carry = jnp.zeros((1, LANES), f32)
    for n in range(seq // ATT_TILE):
        cb = _dot_exact_lhs(tri, lfp[n * ATT_TILE:(n + 1) * ATT_TILE]) + carry
        carry = cb[ATT_TILE - 1:ATT_TILE, :]
        cp_ref[n * ATT_TILE:(n + 1) * ATT_TILE, :] = cb
    cum = cp_ref[...]
    p1 = cum.astype(bf16).astype(f32)
    r1 = cum - p1
    p2 = r1.astype(bf16).astype(f32)
    p3 = (r1 - p2).astype(bf16).astype(f32)
    third = jnp.where(lmod >= 3, lmod - 3, lmod)
    parts = jnp.where(third == 0, p1, jnp.where(third == 1, p2, p3))
    qx = jnp.where(lmod < 3, parts, jnp.where(lmod < 6, 1.0, 0.0))
    kx = jnp.where(lmod < 3, 1.0, jnp.where(lmod < 6, -parts, 0.0))
    return qx, kx


def _moba_extras(sp_ref, seq):
    lane, lmod = _lane_ids()
    sp = sp_ref[0, 0:1, :]
    pos = lax.broadcasted_iota(jnp.int32, (seq, LANES), 0)
    hi = (pos >> 7).astype(f32)
    lo = (pos & 127).astype(f32)
    blk = pos >> 8
    qx = jnp.where(lmod < 6, sp, jnp.where(lmod < 9, -hi, jnp.where(lmod < 12, -lo, 0.0)))
    onehot = jnp.where((lmod >= PEN_LANE0) & (lmod < PEN_LANE0 + 8) & (blk == lmod - PEN_LANE0), 1.0, 0.0)
    kx = jnp.where(lmod < 3, 128.0 * hi,
                   jnp.where(lmod < 6, lo,
                             jnp.where(lmod < 9, 128.0 * sp, jnp.where(lmod < 12, sp, onehot))))
    return qx, kx


def _moba_penalty(gate, own, lane0):
    lane = lax.broadcasted_iota(jnp.int32, (1, LANES), 1)
    lane_f = lane.astype(f32)
    cand = (lane >= lane0) & (lane < lane0 + own)
    g = jnp.where(cand, gate, -3e38)
    sel = jnp.zeros(gate.shape, jnp.bool_)
    for _ in range(MOBA_TOPK):
        mx = jnp.max(g, axis=-1, keepdims=True)
        idx = jnp.min(jnp.where(g == mx, lane_f, 4.0 * LANES), axis=-1, keepdims=True)
        pick = (lane_f == idx) & (mx > -3e38)
        sel = sel | pick
        g = jnp.where(pick, -3e38, g)
    return jnp.where(cand & jnp.logical_not(sel), NEG_BIG, 0.0)


def _flash_kernel(q_ref, k_ref, v_ref, aux_ref, o_ref, qx_ref, ka0_ref, ka1_ref, vb_ref, cp_ref,
                  kmr_ref, *, mode, seq):
    first = _first_head_lanes()
    scale = HEAD_DIM ** -0.5
    if mode == "fox":
        qx, kx = _fox_extras(aux_ref, cp_ref, seq)
    else:
        qx, kx = _moba_extras(aux_ref, seq)
        nblk = seq // MOBA_BLOCK
        kmr_ref[...] = jnp.zeros((LANES, LANES), f32)
        for n in range(nblk):
            km = jnp.sum(k_ref[0, n * MOBA_BLOCK:(n + 1) * MOBA_BLOCK, :], axis=0, keepdims=True) * (1.0 / MOBA_BLOCK)
            r0 = HEAD_DIM + PEN_LANE0 + n
            kmr_ref[r0:r0 + 1, :] = jnp.where(first, km, 0.0)
            kmr_ref[PEN_LANE0 + n:PEN_LANE0 + n + 1, :] = jnp.where(first, 0.0, km)
    qx_ref[...] = qx
    kk = k_ref[0]
    ka0_ref[...] = jnp.where(first, kk, kx).astype(bf16)
    ka1_ref[...] = jnp.where(first, kx, kk).astype(bf16)
    vb_ref[...] = v_ref[0].astype(bf16)

    tq = min(seq, Q_TILE)
    rr = lax.broadcasted_iota(jnp.int32, (tq, tq), 0)
    cc = lax.broadcasted_iota(jnp.int32, (tq, tq), 1)
    causal = cc <= rr
    for qi in range(seq // tq):
        row0 = qi * tq
        q_raw = q_ref[0, row0:row0 + tq, :]
        qxt = qx_ref[row0:row0 + tq, :]
        if mode == "moba":
            gate = _dot_nt(q_raw, kmr_ref[...], precision=lax.Precision.HIGHEST)
            own = (row0 + lax.broadcasted_iota(jnp.int32, (tq, 1), 0)) // MOBA_BLOCK
        outs = []
        for hh in (0, 1):
            own_lanes = first if hh == 0 else jnp.logical_not(first)
            qa = jnp.where(own_lanes, q_raw * scale, qxt)
            if mode == "moba":
                qa = qa + _moba_penalty(gate, own, (HEAD_DIM if hh == 0 else 0) + PEN_LANE0)
            qa = qa.astype(bf16)
            ka_ref = ka0_ref if hh == 0 else ka1_ref
            s_diag = jnp.where(causal, _dot_nt(qa, ka_ref[row0:row0 + tq, :]), NEG_BIG)
            m = jnp.max(s_diag, axis=-1, keepdims=True)
            if row0:
                s_past = _dot_nt(qa, ka_ref[0:row0, :])
                m = jnp.maximum(m, jnp.max(s_past, axis=-1, keepdims=True))
            p = jnp.exp(s_diag - m)
            l = jnp.sum(p, axis=-1, keepdims=True)
            acc = _dot(p.astype(bf16), vb_ref[row0:row0 + tq, :])
            if row0:
                p = jnp.exp(s_past - m)
                l = l + jnp.sum(p, axis=-1, keepdims=True)
                acc = acc + _dot(p.astype(bf16), vb_ref[0:row0, :])
            outs.append(acc / l)
        o_ref[0, row0:row0 + tq, :] = jnp.where(first, outs[0], outs[1])


def _alibi_piece_table():
    slopes = jnp.asarray(2.0 ** (-8.0 * np.arange(1, N_HEADS + 1, dtype=np.float32) / N_HEADS), f32)
    p1, p2, p3 = (p.astype(f32) for p in _split3(slopes))
    pieces = jnp.stack([p1, p2, p3], axis=-1)
    lane = np.arange(LANES)
    lmod = lane % HEAD_DIM
    owner = np.where(lane < HEAD_DIM, 1, 0)
    tab = pieces[2 * np.arange(HEAD_PAIRS)[:, None] + owner[None, :], (lmod % 3)[None, :]]
    tab = jnp.where((lmod < 12)[None, :], tab, 0.0)
    return jnp.broadcast_to(tab[:, None, :], (HEAD_PAIRS, 8, LANES))


def prompt_attention(q, k, v, logf=None):
    nb, seq, d = q.shape
    assert seq % ATT_TILE == 0 and seq % min(seq, Q_TILE) == 0 and seq // MOBA_BLOCK <= 8 and seq <= 128 * 256
    mode = "moba" if logf is None else "fox"
    qkv_spec = pl.BlockSpec((1, seq, LANES), lambda b, hp: (b, 0, hp))
    if mode == "fox":
        aux, aux_spec = logf, pl.BlockSpec((1, seq, N_HEADS), lambda b, hp: (b, 0, 0))
    else:
        aux, aux_spec = _alibi_piece_table(), pl.BlockSpec((1, 8, LANES), lambda b, hp: (hp, 0, 0))
    return pl.pallas_call(
        functools.partial(_flash_kernel, mode=mode, seq=seq),
        grid=(nb, HEAD_PAIRS),
        in_specs=[qkv_spec, qkv_spec, qkv_spec, aux_spec],
        out_specs=qkv_spec,
        out_shape=jax.ShapeDtypeStruct((nb, seq, d), f32),
        scratch_shapes=[pltpu.VMEM((seq, LANES), f32), pltpu.VMEM((seq, LANES), bf16),
                        pltpu.VMEM((seq, LANES), bf16), pltpu.VMEM((seq, LANES), bf16),
                        pltpu.VMEM((seq, LANES), f32), pltpu.VMEM((LANES, LANES), f32)],
        compiler_params=_cparams("parallel", "parallel"),
        name=f"{mode}_prompt_attn",
    )(q, k, v, aux)


def _rms_kernel(x_ref, ng_ref, h_ref):
    x = x_ref[...]
    ms = jnp.mean(x * x, axis=-1, keepdims=True)
    h_ref[...] = x * lax.rsqrt(ms + RMS_EPS) * ng_ref[...]


def rms_norm_rows(x2d, norm_g):
    n, d = x2d.shape
    tm = _row_tile(n)
    row = lambda i: (i, 0)
    return pl.pallas_call(
        _rms_kernel, grid=(n // tm,),
        in_specs=[pl.BlockSpec((tm, d), row), pl.BlockSpec((1, d), lambda i: (0, 0))],
        out_specs=pl.BlockSpec((tm, d), row),
        out_shape=jax.ShapeDtypeStruct((n, d), f32),
        compiler_params=_cparams("parallel"), name="rms_norm",
    )(x2d, norm_g.reshape(1, d))


def _rwkv_proj_kernel(h_ref, hp_ref, mu_ref, w_ref, w0_ref, w1_ref, w2_ref, a0_ref, a1_ref, a2_ref,
                      kk_ref, ka_ref, rk_ref,
                      r_ref, dec_ref, k_ref, v_ref, na_ref, b_ref, g_ref, bonus_ref):
    d = D_MODEL
    h = h_ref[...]
    delta = hp_ref[...] - h

    def mixed(i):
        return (h + delta * mu_ref[i:i + 1, :]).astype(bf16)

    r = _dot(mixed(0), w_ref[:, 0:d])
    k = _dot(mixed(1), w_ref[:, d:2 * d])
    v = _dot(mixed(2), w_ref[:, 2 * d:3 * d])
    g_ref[...] = _dot(mixed(3), w_ref[:, 3 * d:4 * d])
    ww = w0_ref[...] + _dot(jnp.tanh(_dot(mixed(4), w1_ref[...])).astype(bf16), w2_ref[...])
    dec_ref[...] = jnp.exp(-jnp.exp(_log_sigmoid(ww) - 0.5))
    aa = jax.nn.sigmoid(a0_ref[...] + _dot(_dot(mixed(5), a1_ref[...]).astype(bf16), a2_ref[...]))
    k2 = k * (1.0 + (aa - 1.0) * ka_ref[...])
    kkv = k * kk_ref[...]
    rkk = r * k2 * rk_ref[...]
    r_ref[...] = r
    k_ref[...] = k2
    v_ref[...] = v
    for c in range(d // LANES):
        sl = slice(c * LANES, (c + 1) * LANES)
        kc = kkv[:, sl]
        kn = kc / jnp.maximum(jnp.sqrt(_per_head_sum(kc * kc)), 1e-12)
        na_ref[:, sl] = -kn
        b_ref[:, sl] = kn * aa[:, sl]
        bonus_ref[:, sl] = _per_head_sum(rkk[:, sl]) * v[:, sl]


def rwkv_project(h2d, hprev2d, mu, w_in, w0, w1, w2, a0, a1, a2, k_k, k_a, r_k):
    n, d = h2d.shape
    tm = min(_row_tile(n), 256)
    row = lambda i: (i, 0)
    const = lambda i: (0, 0)
    lw, la = w1.shape[1], a1.shape[1]
    vec = lambda z: z.reshape(1, d)
    big = jax.ShapeDtypeStruct((n, d), f32)
    return pl.pallas_call(
        _rwkv_proj_kernel, grid=(n // tm,),
        in_specs=[pl.BlockSpec((tm, d), row), pl.BlockSpec((tm, d), row), pl.BlockSpec((6, d), const),
                  pl.BlockSpec((d, 4 * d), const), pl.BlockSpec((1, d), const),
                  pl.BlockSpec((d, lw), const), pl.BlockSpec((lw, d), const), pl.BlockSpec((1, d), const),
                  pl.BlockSpec((d, la), const), pl.BlockSpec((la, d), const),
                  pl.BlockSpec((1, d), const), pl.BlockSpec((1, d), const), pl.BlockSpec((1, d), const)],
        out_specs=[pl.BlockSpec((tm, d), row)] * 8,
        out_shape=[big] * 8,
        compiler_params=_cparams("parallel"), name="rwkv_proj",
    )(h2d, hprev2d, mu, w_in.astype(bf16), vec(w0), w1.astype(bf16), w2.astype(bf16), vec(a0),
      a1.astype(bf16), a2.astype(bf16), vec(k_k), vec(k_a), vec(r_k))


SCAN_CHUNK = 32


def _rwkv_scan_kernel(r_ref, w_ref, k_ref, v_ref, na_ref, b_ref, s0_ref, y_ref, st_ref, *, tc):
    @pl.when(pl.program_id(1) == 0)
    def _():
        st_ref[...] = s0_ref[...]

    zeros = jnp.zeros((HEAD_DIM, LANES), f32)

    def step(t, carry):
        def sa_pass(j, sa):
            return sa + st_ref[j] * na_ref[t, pl.ds(j, 1), :]

        sa = lax.fori_loop(0, HEAD_DIM, sa_pass, zeros, unroll=8)
        vt = v_ref[t]

        def update_pass(j, y):
            s_new = (st_ref[j] * w_ref[t, pl.ds(j, 1), :] + sa * b_ref[t, pl.ds(j, 1), :]
                     + vt * k_ref[t, pl.ds(j, 1), :])
            st_ref[j] = s_new
            return y + s_new * r_ref[t, pl.ds(j, 1), :]

        y_ref[t] = lax.fori_loop(0, HEAD_DIM, update_pass, zeros, unroll=8)
        return carry

    lax.fori_loop(0, tc, step, 0)


def rwkv_scan(r, w, k, v, na, b, s0):
    nb, t, d = r.shape
    lanes = nb * N_HEADS
    assert lanes % LANES == 0
    tc = min(t, SCAN_CHUNK)
    assert t % tc == 0

    def to_scan(z):
        return z.reshape(nb, t, N_HEADS, HEAD_DIM).transpose(1, 3, 0, 2).reshape(t, HEAD_DIM, lanes)

    s0_l = s0.transpose(3, 2, 0, 1).reshape(HEAD_DIM, HEAD_DIM, lanes)
    seq_spec = pl.BlockSpec((tc, HEAD_DIM, LANES), lambda g, i: (i, 0, g))
    st_spec = pl.BlockSpec((HEAD_DIM, HEAD_DIM, LANES), lambda g, i: (0, 0, g))
    y, st = pl.pallas_call(
        functools.partial(_rwkv_scan_kernel, tc=tc),
        grid=(lanes // LANES, t // tc),
        in_specs=[seq_spec] * 6 + [st_spec],
        out_specs=[seq_spec, st_spec],
        out_shape=[jax.ShapeDtypeStruct((t, HEAD_DIM, lanes), f32),
                   jax.ShapeDtypeStruct((HEAD_DIM, HEAD_DIM, lanes), f32)],
        compiler_params=_cparams("parallel", "arbitrary"), name="rwkv_scan",
    )(*(to_scan(z) for z in (r, w, k, v, na, b)), s0_l)
    y = y.reshape(t, HEAD_DIM, nb, N_HEADS).transpose(2, 0, 3, 1).reshape(nb, t, d)
    st = st.reshape(HEAD_DIM, HEAD_DIM, nb, N_HEADS).transpose(2, 3, 1, 0)
    return y, st


def _rwkv_post_kernel(x_ref, y_ref, bonus_ref, g_ref, lng_ref, lnb_ref, w_ref, o_ref, a_ref):
    for c in range(D_MODEL // LANES):
        sl = slice(c * LANES, (c + 1) * LANES)
        y = y_ref[:, sl]
        dev = y - _per_head_sum(y) * (1.0 / HEAD_DIM)
        var = _per_head_sum(dev * dev) * (1.0 / HEAD_DIM)
        yn = dev * lax.rsqrt(var + GN_EPS) * lng_ref[:, sl] + lnb_ref[:, sl] + bonus_ref[:, sl]
        g = g_ref[:, sl]
        a_ref[:, sl] = (yn * (g * jax.nn.sigmoid(g))).astype(bf16)
    o_ref[...] = x_ref[...] + _dot(a_ref[...], w_ref[...])


def rwkv_post(x2d, y2d, bonus2d, g2d, ln_g, ln_b, w_out):
    n, d = x2d.shape
    tm = _row_tile(n)
    row = lambda i: (i, 0)
    const = lambda i: (0, 0)
    return pl.pallas_call(
        _rwkv_post_kernel, grid=(n // tm,),
        in_specs=[pl.BlockSpec((tm, d), row)] * 4 + [pl.BlockSpec((1, d), const)] * 2 + [pl.BlockSpec((d, d), const)],
        out_specs=pl.BlockSpec((tm, d), row),
        out_shape=jax.ShapeDtypeStruct((n, d), f32),
        scratch_shapes=[pltpu.VMEM((tm, d), bf16)],
        compiler_params=_cparams("parallel"), name="rwkv_post",
    )(x2d, y2d, bonus2d, g2d, ln_g.reshape(1, d), ln_b.reshape(1, d), w_out.astype(bf16))


def rwkv_layer(x, x_last, s0, norm_g, mu, w_in, w0, w1, w2, a0, a1, a2, k_k, k_a, r_k, ln_g, ln_b, w_out):
    nb, t, d = x.shape
    x2d = x.reshape(nb * t, d)
    hn = rms_norm_rows(x2d, norm_g).reshape(nb, t, d)
    hprev = jnp.concatenate([x_last[:, None, :], hn[:, :-1]], axis=1)
    r, dec, k, v, na, b, g, bonus = rwkv_project(hn.reshape(nb * t, d), hprev.reshape(nb * t, d), mu, w_in,
                                                 w0, w1, w2, a0, a1, a2, k_k, k_a, r_k.reshape(-1))
    shp = (nb, t, d)
    y, st = rwkv_scan(*(z.reshape(shp) for z in (r, dec, k, v, na, b)), s0)
    out = rwkv_post(x2d, y.reshape(nb * t, d), bonus, g, ln_g, ln_b, w_out)
    return out.reshape(nb, t, d), st, hn[:, -1]


PAGES_PER_STEP = 8


def _pages_as_matrices(pool):
    moved = jnp.moveaxis(pool, 2, -1)
    return moved.reshape(moved.shape[:2] + (-1, moved.shape[-1]))


def _page_specs(pool, layer, n_arrays=PAGES_PER_STEP):
    blk = (1, 1) + pool.shape[2:]
    zeros = (0,) * (pool.ndim - 2)
    return [pl.BlockSpec(blk, functools.partial(
        lambda b, jg, pt, p: (layer, pt[b, jg * PAGES_PER_STEP + p]) + zeros, p=p)) for p in range(n_arrays)]


def _block_diag_queries(qh):
    nb, _, t, _ = qh.shape
    eye = jnp.eye(N_HEADS, dtype=qh.dtype)
    return jnp.einsum("bhtd,hg->bhtgd", qh, eye).reshape(nb, N_HEADS * t, D_MODEL).astype(bf16)


def _page_scores(qbd_ref, k_ref):
    return _dot(qbd_ref[0], k_ref[0, 0].astype(bf16))


def _own_head_block(full, t):
    head = lax.broadcasted_iota(jnp.int32, (N_HEADS * t, 1), 0) // t
    pair = jnp.zeros((N_HEADS * t, LANES), f32)
    for c in range(HEAD_PAIRS):
        pair = pair + jnp.where((head >> 1) == c, full[:, c * LANES:(c + 1) * LANES], 0.0)
    return jnp.where((head & 1) == 0, pair[:, :HEAD_DIM], pair[:, HEAD_DIM:])


def _softmax_pages(scores, v_refs, t, m_old=None):
    mx = scores[0]
    for s in scores[1:]:
        mx = jnp.maximum(mx, s)
    m_new = jnp.broadcast_to(jnp.max(mx, axis=-1, keepdims=True), mx.shape)
    if m_old is not None:
        m_new = jnp.maximum(m_new, m_old)
    probs = [jnp.exp(s - m_new) for s in scores]
    psum = probs[0]
    for p in probs[1:]:
        psum = psum + p
    l_new = jnp.broadcast_to(jnp.sum(psum, axis=-1, keepdims=True), psum.shape)
    full = _dot_nt(probs[0].astype(bf16), v_refs[0][0, 0].astype(bf16))
    for p, v_ref in zip(probs[1:], v_refs[1:]):
        full = full + _dot_nt(p.astype(bf16), v_ref[0, 0].astype(bf16))
    return m_new, l_new, _own_head_block(full, t)


def _head_rows_selector(t):
    r = lax.broadcasted_iota(jnp.int32, (N_HEADS * t, N_HEADS), 0)
    c = lax.broadcasted_iota(jnp.int32, (N_HEADS * t, N_HEADS), 1)
    return jnp.where((r >= c * t) & (r < (c + 1) * t), 1.0, 0.0).astype(bf16)


def _fox_decode_kernel(pt_ref, q_ref, *refs):
    k_refs = refs[0:PAGES_PER_STEP]
    v_refs = refs[PAGES_PER_STEP:2 * PAGES_PER_STEP]
    lf_refs = refs[2 * PAGES_PER_STEP:3 * PAGES_PER_STEP]
    m_ref, l_ref, acc_ref, tot_ref = refs[3 * PAGES_PER_STEP:]
    t = q_ref.shape[1] // N_HEADS

    @pl.when(pl.program_id(1) == 0)
    def _():
        m_ref[...] = jnp.full(m_ref.shape, NEG_BIG, f32)
        l_ref[...] = jnp.zeros(l_ref.shape, f32)
        acc_ref[...] = jnp.zeros(acc_ref.shape, f32)
        tot_ref[...] = jnp.zeros(tot_ref.shape, f32)

    scale = HEAD_DIM ** -0.5
    r = lax.broadcasted_iota(jnp.int32, (PAGE_SIZE, PAGE_SIZE), 0)
    c = lax.broadcasted_iota(jnp.int32, (PAGE_SIZE, PAGE_SIZE), 1)
    upper = jnp.where(r <= c, 1.0, 0.0).astype(bf16)
    repeat_rows = _head_rows_selector(t)
    tot = tot_ref[0]
    scores = []
    for p in range(PAGES_PER_STEP):
        l1, l2, l3 = _split3(lf_refs[p][0, 0])
        local = _dot(l1, upper) + _dot(l2, upper) + _dot(l3, upper)
        cum = _dot_exact_lhs(repeat_rows, tot + local)
        tot = tot + jnp.broadcast_to(local[:, PAGE_SIZE - 1:PAGE_SIZE], local.shape)
        scores.append(_page_scores(q_ref, k_refs[p]) * scale - cum)
    tot_ref[0] = tot
    m_old = m_ref[0]
    m_new, l_new, acc = _softmax_pages(scores, v_refs, t, m_old)
    alpha = jnp.exp(m_old - m_new)
    m_ref[0] = m_new
    l_ref[0] = alpha * l_ref[0] + l_new
    acc_ref[0] = alpha[:, :HEAD_DIM] * acc_ref[0] + acc


def _fox_finish_kernel(q_ref, k_ref, v_ref, lf_ref, lft_ref, m_ref, l_ref, acc_ref, tot_ref, o_ref):
    t = q_ref.shape[2]
    scale = HEAD_DIM ** -0.5
    r = lax.broadcasted_iota(jnp.int32, (t, t), 0)
    c = lax.broadcasted_iota(jnp.int32, (t, t), 1)
    lower = jnp.where(r >= c, 1.0, 0.0).astype(bf16)
    upper = jnp.where(r <= c, 1.0, 0.0).astype(bf16)
    cn_col = _dot_exact_lhs(lower, lf_ref[0])
    l1, l2, l3 = _split3(lft_ref[0])
    cn_row = _dot(l1, upper) + _dot(l2, upper) + _dot(l3, upper)
    for h in range(N_HEADS):
        rows = slice(h * t, (h + 1) * t)
        cq = cn_col[:, h:h + 1]
        s = _dot_nt(q_ref[0, h], k_ref[0, h]) * scale + cq - cn_row[h:h + 1, :]
        s = jnp.where(c <= r, s, NEG_BIG)
        m_past = m_ref[0, rows, 0:1] + cq + tot_ref[0, h:h + 1, 0:1]
        m_tot = jnp.maximum(m_past, jnp.max(s, axis=-1, keepdims=True))
        w_past = jnp.exp(m_past - m_tot)
        p = jnp.exp(s - m_tot)
        l_tot = w_past * l_ref[0, rows, 0:1] + jnp.sum(p, axis=-1, keepdims=True)
        o_ref[0, h] = (w_past * acc_ref[0, rows, :] + _dot(p, v_ref[0, h])) / l_tot


def _heads_first(z2d, nb, t):
    return z2d.reshape(nb, t, N_HEADS, HEAD_DIM).transpose(0, 2, 1, 3)


def _heads_last(z4d):
    nb, _, t, _ = z4d.shape
    return z4d.transpose(0, 2, 1, 3).reshape(nb * t, D_MODEL)


def fox_decode_attention(q2d, k2d, v2d, logf2d, pool_k, pool_v, pool_lf, layer, page_table):
    nb, n_pages = page_table.shape
    t = q2d.shape[0] // nb
    assert n_pages % PAGES_PER_STEP == 0
    qh, kh, vh = (_heads_first(z, nb, t) for z in (q2d, k2d, v2d))
    lf = logf2d.reshape(nb, t, N_HEADS)
    pool_k, pool_v = _pages_as_matrices(pool_k), _pages_as_matrices(pool_v)
    pool_lft = jnp.swapaxes(pool_lf, 2, 3)
    rows = N_HEADS * t
    head_spec = pl.BlockSpec((1, rows, D_MODEL), lambda b, jg, pt: (b, 0, 0))
    stat_spec = pl.BlockSpec((1, rows, PAGE_SIZE), lambda b, jg, pt: (b, 0, 0))
    acc_spec = pl.BlockSpec((1, rows, HEAD_DIM), lambda b, jg, pt: (b, 0, 0))
    tot_spec = pl.BlockSpec((1, N_HEADS, PAGE_SIZE), lambda b, jg, pt: (b, 0, 0))
    stat = jax.ShapeDtypeStruct((nb, rows, PAGE_SIZE), f32)
    m, l, acc, tot = pl.pallas_call(
        _fox_decode_kernel,
        grid_spec=pltpu.PrefetchScalarGridSpec(
            num_scalar_prefetch=1, grid=(nb, n_pages // PAGES_PER_STEP),
            in_specs=[head_spec] + _page_specs(pool_k, layer) + _page_specs(pool_v, layer) + _page_specs(pool_lft, layer),
            out_specs=[stat_spec, stat_spec, acc_spec, tot_spec]),
        out_shape=[stat, stat, jax.ShapeDtypeStruct((nb, rows, HEAD_DIM), f32),
                   jax.ShapeDtypeStruct((nb, N_HEADS, PAGE_SIZE), f32)],
        compiler_params=_cparams("parallel", "arbitrary"), name="fox_decode",
    )(page_table, _block_diag_queries(qh), *([pool_k] * PAGES_PER_STEP), *([pool_v] * PAGES_PER_STEP), *([pool_lft] * PAGES_PER_STEP))
    hs = pl.BlockSpec((1, N_HEADS, t, HEAD_DIM), lambda b: (b, 0, 0, 0))
    ss = pl.BlockSpec((1, rows, PAGE_SIZE), lambda b: (b, 0, 0))
    o = pl.pallas_call(
        _fox_finish_kernel, grid=(nb,),
        in_specs=[hs, hs, hs, pl.BlockSpec((1, t, N_HEADS), lambda b: (b, 0, 0)),
                  pl.BlockSpec((1, N_HEADS, t), lambda b: (b, 0, 0)), ss, ss,
                  pl.BlockSpec((1, rows, HEAD_DIM), lambda b: (b, 0, 0)),
                  pl.BlockSpec((1, N_HEADS, PAGE_SIZE), lambda b: (b, 0, 0))],
        out_specs=hs, out_shape=jax.ShapeDtypeStruct((nb, N_HEADS, t, HEAD_DIM), f32),
        compiler_params=_cparams("parallel"), name="fox_decode_finish",
    )(qh, kh, vh, lf, jnp.swapaxes(lf, 1, 2), m, l, acc, tot)
    return _heads_last(o)


def _alibi_slopes_np():
    return 2.0 ** (-8.0 * np.arange(1, N_HEADS + 1, dtype=np.float32) / N_HEADS)


def _moba_decode_kernel(pt_ref, q_ref, qt_ref, *refs, past_len):
    k_refs = refs[0:PAGES_PER_STEP]
    v_refs = refs[PAGES_PER_STEP:2 * PAGES_PER_STEP]
    slope_ref, m_ref, l_ref, acc_ref, gate_ref = refs[2 * PAGES_PER_STEP:]
    rows = q_ref.shape[1]
    t = rows // N_HEADS
    scale = HEAD_DIM ** -0.5
    pages_per_block = MOBA_BLOCK // PAGE_SIZE
    page0 = pl.program_id(1) * PAGES_PER_STEP
    qpos = past_len + lax.broadcasted_iota(jnp.int32, (rows, PAGE_SIZE), 0) % t
    lane = lax.broadcasted_iota(jnp.int32, (rows, PAGE_SIZE), 1)
    for blk in range(PAGES_PER_STEP // pages_per_block):
        pages = range(blk * pages_per_block, (blk + 1) * pages_per_block)
        kblock = k_refs[pages[0]][0, 0]
        for p in pages[1:]:
            kblock = kblock + k_refs[p][0, 0]
        kmean = jnp.sum(kblock, axis=-1, keepdims=True) * (1.0 / MOBA_BLOCK)
        for h in range(N_HEADS):
            kmean_h = kmean[h * HEAD_DIM:(h + 1) * HEAD_DIM, :]
            gate_ref[0, blk, h:h + 1, :] = jnp.sum(qt_ref[0, h] * kmean_h, axis=0, keepdims=True)
        scores = []
        for p in pages:
            dist = (qpos - ((page0 + p) * PAGE_SIZE + lane)).astype(f32)
            scores.append(_page_scores(q_ref, k_refs[p]) * scale - slope_ref[...] * dist)
        m, l, acc = _softmax_pages(scores, [v_refs[p] for p in pages], t)
        m_ref[0, blk] = m[:, :HEAD_DIM]
        l_ref[0, blk] = l[:, :HEAD_DIM]
        acc_ref[0, blk] = acc


def _moba_finish_kernel(q_ref, k_ref, v_ref, m_ref, l_ref, acc_ref, gate_ref, o_ref):
    t = q_ref.shape[2]
    n_blocks = m_ref.shape[1]
    scale = HEAD_DIM ** -0.5
    slopes = _alibi_slopes_np()
    r = lax.broadcasted_iota(jnp.int32, (t, t), 0)
    c = lax.broadcasted_iota(jnp.int32, (t, t), 1)
    lane = lax.broadcasted_iota(jnp.int32, (1, n_blocks), 1).astype(f32)
    g = gate_ref[0]
    sel = jnp.zeros(g.shape, f32)
    for _ in range(min(MOBA_TOPK, n_blocks)):
        mx = jnp.max(g, axis=-1, keepdims=True)
        idx = jnp.min(jnp.where(g == mx, lane, 4.0 * n_blocks), axis=-1, keepdims=True)
        pick = lane == idx
        sel = jnp.where(pick, 1.0, sel)
        g = jnp.where(pick, -3e38, g)
    for h in range(N_HEADS):
        qh = q_ref[0, h]
        s = _dot_nt(qh, k_ref[0, h]) * scale - float(slopes[h]) * (r - c).astype(f32)
        s = jnp.where(c <= r, s, NEG_BIG)
        m_own = jnp.max(s, axis=-1, keepdims=True)
        p = jnp.exp(s - m_own)
        l_own = jnp.sum(p, axis=-1, keepdims=True)
        acc_own = _dot(p, v_ref[0, h])
        rows = slice(h * t, (h + 1) * t)
        sel_h = sel[rows, :]
        chosen = [jnp.broadcast_to(sel_h[:, n:n + 1], (t, HEAD_DIM)) for n in range(n_blocks)]
        m_blk = [jnp.where(chosen[n] > 0.0, m_ref[0, n, rows, :], NEG_BIG) for n in range(n_blocks)]
        m_tot = jnp.broadcast_to(m_own, (t, HEAD_DIM))
        for n in range(n_blocks):
            m_tot = jnp.maximum(m_tot, m_blk[n])
        w_own = jnp.exp(m_own - m_tot)
        l_tot = w_own * l_own
        acc_tot = w_own * acc_own
        for n in range(n_blocks):
            w = jnp.exp(m_blk[n] - m_tot) * chosen[n]
            l_tot = l_tot + w * l_ref[0, n, rows, :]
            acc_tot = acc_tot + w * acc_ref[0, n, rows, :]
        o_ref[0, h] = acc_tot / l_tot


def moba_decode_attention(q2d, k2d, v2d, pool_k, pool_v, layer, page_table):
    nb, n_pages = page_table.shape
    t = q2d.shape[0] // nb
    past_len = n_pages * PAGE_SIZE
    assert past_len % MOBA_BLOCK == 0 and t <= MOBA_BLOCK and n_pages % PAGES_PER_STEP == 0
    n_blocks = past_len // MOBA_BLOCK
    bps = PAGES_PER_STEP * PAGE_SIZE // MOBA_BLOCK
    qh, kh, vh = (_heads_first(z, nb, t) for z in (q2d, k2d, v2d))
    qt = jnp.swapaxes(qh, 2, 3)
    pool_k, pool_v = _pages_as_matrices(pool_k), _pages_as_matrices(pool_v)
    rows = N_HEADS * t
    head_spec = pl.BlockSpec((1, rows, D_MODEL), lambda b, jg, pt: (b, 0, 0))
    qt_spec = pl.BlockSpec((1, N_HEADS, HEAD_DIM, t), lambda b, jg, pt: (b, 0, 0, 0))
    slope_rows = jnp.broadcast_to(jnp.repeat(jnp.asarray(_alibi_slopes_np()), t)[:, None], (rows, PAGE_SIZE))
    part_spec = pl.BlockSpec((1, bps, rows, HEAD_DIM), lambda b, jg, pt: (b, jg, 0, 0))
    part = jax.ShapeDtypeStruct((nb, n_blocks, rows, HEAD_DIM), f32)
    m, l, acc, gate = pl.pallas_call(
        functools.partial(_moba_decode_kernel, past_len=past_len),
        grid_spec=pltpu.PrefetchScalarGridSpec(
            num_scalar_prefetch=1, grid=(nb, n_pages // PAGES_PER_STEP),
            in_specs=([head_spec, qt_spec] + _page_specs(pool_k, layer) + _page_specs(pool_v, layer)
                      + [pl.BlockSpec((rows, PAGE_SIZE), lambda b, jg, pt: (0, 0))]),
            out_specs=[part_spec, part_spec, part_spec,
                       pl.BlockSpec((1, bps, N_HEADS, t), lambda b, jg, pt: (b, jg, 0, 0))]),
        out_shape=[part, part, part, jax.ShapeDtypeStruct((nb, n_blocks, N_HEADS, t), f32)],
        compiler_params=_cparams("parallel", "parallel"), name="moba_decode",
    )(page_table, _block_diag_queries(qh), qt, *([pool_k] * PAGES_PER_STEP), *([pool_v] * PAGES_PER_STEP), slope_rows)
    gate = gate.transpose(0, 2, 3, 1).reshape(nb, rows, n_blocks)
    hs = pl.BlockSpec((1, N_HEADS, t, HEAD_DIM), lambda b: (b, 0, 0, 0))
    ps = pl.BlockSpec((1, n_blocks, rows, HEAD_DIM), lambda b: (b, 0, 0, 0))
    o = pl.pallas_call(
        _moba_finish_kernel, grid=(nb,),
        in_specs=[hs, hs, hs, ps, ps, ps, pl.BlockSpec((1, N_HEADS * t, n_blocks), lambda b: (b, 0, 0))],
        out_specs=hs, out_shape=jax.ShapeDtypeStruct((nb, N_HEADS, t, HEAD_DIM), f32),
        compiler_params=_cparams("parallel"), name="moba_decode_finish",
    )(qh, kh, vh, m, l, acc, gate)
    return _heads_last(o)


N_MIXERS = 3


def kernel(x_prompt, x_sample, cache_fox_k, cache_fox_v, cache_fox_logf, state_rwkv_wkv, state_rwkv_shift, cache_moba_k, cache_moba_v, page_table, norm_g, fox_w_in, fox_b_f, fox_q_g, fox_k_g, fox_w_out, rwkv_mu, rwkv_w_in, rwkv_w0, rwkv_w1, rwkv_w2, rwkv_a0, rwkv_a1, rwkv_a2, rwkv_k_k, rwkv_k_a, rwkv_r_k, rwkv_ln_g, rwkv_ln_b, rwkv_w_out, moba_w_in, moba_q_g, moba_k_g, moba_w_out):
    nb, seq, d = x_prompt.shape
    db, dseq, _ = x_sample.shape
    depth = norm_g.shape[0]
    xp = x_prompt.reshape(nb * seq, d)
    xs = x_sample.reshape(db * dseq, d)
    heads_p = (nb, seq, N_HEADS, HEAD_DIM)
    heads_s = (db, dseq, N_HEADS, HEAD_DIM)
    outs = {name: [] for name in ("fk_p", "fv_p", "flf_p", "fk_s", "fv_s", "flf_s", "rw_p", "rsh_p", "rw_s", "rsh_s",
                                  "mk_p", "mv_p", "mk_s", "mv_s")}
    for i in range(depth):
        kind, j = i % N_MIXERS, i // N_MIXERS
        if kind == 0:
            args = (norm_g[i], fox_w_in[j], fox_q_g[j], fox_k_g[j], fox_b_f[j])
            qp, kp, vp, gp, lfp = attn_project(xp, *args)
            qs, ks, vs, gs, lfs = attn_project(xs, *args)
            op = prompt_attention(qp.reshape(nb, seq, d), kp.reshape(nb, seq, d), vp.reshape(nb, seq, d),
                                  lfp.reshape(nb, seq, N_HEADS)).reshape(nb * seq, d)
            os_ = fox_decode_attention(qs, ks, vs, lfs, cache_fox_k, cache_fox_v, cache_fox_logf, j, page_table)
            xp = out_project(xp, op, gp, fox_w_out[j])
            xs = out_project(xs, os_, gs, fox_w_out[j])
            outs["fk_p"].append(kp.reshape(heads_p)); outs["fv_p"].append(vp.reshape(heads_p))
            outs["flf_p"].append(lfp.reshape(nb, seq, N_HEADS))
            outs["fk_s"].append(ks.reshape(heads_s)); outs["fv_s"].append(vs.reshape(heads_s))
            outs["flf_s"].append(lfs.reshape(db, dseq, N_HEADS))
        elif kind == 1:
            wts = (norm_g[i], rwkv_mu[j], rwkv_w_in[j], rwkv_w0[j], rwkv_w1[j], rwkv_w2[j], rwkv_a0[j], rwkv_a1[j],
                   rwkv_a2[j], rwkv_k_k[j], rwkv_k_a[j], rwkv_r_k[j], rwkv_ln_g[j], rwkv_ln_b[j], rwkv_w_out[j])
            xp3, st_p, sh_p = rwkv_layer(xp.reshape(nb, seq, d), jnp.zeros((nb, d), f32),
                                         jnp.zeros((nb, N_HEADS, HEAD_DIM, HEAD_DIM), f32), *wts)
            xs3, st_s, sh_s = rwkv_layer(xs.reshape(db, dseq, d), state_rwkv_shift[j], state_rwkv_wkv[j], *wts)
            xp, xs = xp3.reshape(nb * seq, d), xs3.reshape(db * dseq, d)
            outs["rw_p"].append(st_p); outs["rsh_p"].append(sh_p)
            outs["rw_s"].append(st_s); outs["rsh_s"].append(sh_s)
        else:
            args = (norm_g[i], moba_w_in[j], moba_q_g[j], moba_k_g[j])
            qp, kp, vp, gp, _ = attn_project(xp, *args)
            qs, ks, vs, gs, _ = attn_project(xs, *args)
            op = prompt_attention(qp.reshape(nb, seq, d), kp.reshape(nb, seq, d),
                                  vp.reshape(nb, seq, d)).reshape(nb * seq, d)
            os_ = moba_decode_attention(qs, ks, vs, cache_moba_k, cache_moba_v, j, page_table)
            xp = out_project(xp, op, gp, moba_w_out[j])
            xs = out_project(xs, os_, gs, moba_w_out[j])
            outs["mk_p"].append(kp.reshape(heads_p)); outs["mv_p"].append(vp.reshape(heads_p))
            outs["mk_s"].append(ks.reshape(heads_s)); outs["mv_s"].append(vs.reshape(heads_s))
    stacked = {name: jnp.stack(v) for name, v in outs.items()}
    return (xp.reshape(nb, seq, d), xs.reshape(db, dseq, d),
            stacked["fk_p"], stacked["fv_p"], stacked["flf_p"], stacked["fk_s"], stacked["fv_s"], stacked["flf_s"],
            stacked["rw_p"], stacked["rsh_p"], stacked["rw_s"], stacked["rsh_s"],
            stacked["mk_p"], stacked["mv_p"], stacked["mk_s"], stacked["mv_s"])
```

```python
import functools

import numpy as np
import jax
import jax.numpy as jnp
from jax import lax
from jax.experimental import pallas as pl
from jax.experimental.pallas import tpu as pltpu

D_MODEL = 1024
N_HEADS = 16
HEAD_DIM = 64
LANES = 128
HEAD_PAIRS = N_HEADS // 2
PAGE_SIZE = 128
MOBA_BLOCK = 256
MOBA_TOPK = 3
RMS_EPS = 1e-6
GN_EPS = 64e-5
NEG_BIG = -1e30
VMEM_LIMIT = 56 * 1024 * 1024

f32 = jnp.float32
bf16 = jnp.bfloat16


def _cparams(*sem):
    return pltpu.CompilerParams(dimension_semantics=sem, vmem_limit_bytes=VMEM_LIMIT)


def _split3(x):
    p1 = x.astype(bf16)
    r1 = x - p1.astype(f32)
    p2 = r1.astype(bf16)
    p3 = (r1 - p2.astype(f32)).astype(bf16)
    return p1, p2, p3


def _dot(a, b):
    return jnp.dot(a, b, preferred_element_type=f32)


def _dot_nt(a, b, precision=None):
    return lax.dot_general(a, b, (((1,), (1,)), ((), ())), preferred_element_type=f32, precision=precision)


def _dot_exact_lhs(a_bf16, x):
    p1, p2, p3 = _split3(x)
    return _dot(a_bf16, p1) + _dot(a_bf16, p2) + _dot(a_bf16, p3)


def _log_sigmoid(z):
    return jnp.minimum(z, 0.0) - jnp.log1p(jnp.exp(-jnp.abs(z)))


def _first_head_lanes():
    return lax.broadcasted_iota(jnp.int32, (1, LANES), 1) < HEAD_DIM


def _per_head_sum(x):
    first = _first_head_lanes()
    s0 = jnp.sum(jnp.where(first, x, 0.0), axis=-1, keepdims=True)
    s1 = jnp.sum(jnp.where(first, 0.0, x), axis=-1, keepdims=True)
    return jnp.where(first, s0, s1)


def _attn_proj_kernel(x_ref, ng_ref, w_ref, wf_ref, bf_ref, qg_ref, kg_ref,
                      q_ref, k_ref, v_ref, g_ref, lf_ref):
    x = x_ref[...]
    ms = jnp.mean(x * x, axis=-1, keepdims=True)
    hn = (x * lax.rsqrt(ms + RMS_EPS) * ng_ref[...]).astype(bf16)
    d = D_MODEL
    for sec, (out_ref, gain_ref) in enumerate(((q_ref, qg_ref), (k_ref, kg_ref))):
        p = _dot(hn, w_ref[:, sec * d:(sec + 1) * d])
        for c in range(d // LANES):
            blk = p[:, c * LANES:(c + 1) * LANES]
            msq = _per_head_sum(blk * blk) * (1.0 / HEAD_DIM)
            out_ref[:, c * LANES:(c + 1) * LANES] = blk * lax.rsqrt(msq + RMS_EPS) * gain_ref[...]
    v_ref[...] = _dot(hn, w_ref[:, 2 * d:3 * d])
    g_ref[...] = _dot(hn, w_ref[:, 3 * d:4 * d])
    lf = _log_sigmoid(_dot(hn, wf_ref[...]) + bf_ref[...])
    lf_ref[...] = lf[:, :N_HEADS]


def _row_tile(n):
    for t in (512, 256, 128, 64, 32, 16, 8):
        if n % t == 0:
            return t
    raise ValueError(f"row count {n} is not a multiple of 8")


def attn_project(x2d, norm_g, w_in, q_g, k_g, b_f=None):
    n, d = x2d.shape
    tm = _row_tile(n)
    w_main = w_in[:, :4 * d].astype(bf16)
    if b_f is None:
        wf = jnp.zeros((d, LANES), bf16)
        bfp = jnp.zeros((1, LANES), f32)
    else:
        wf = jnp.pad(w_in[:, 4 * d:], ((0, 0), (0, LANES - N_HEADS))).astype(bf16)
        bfp = jnp.pad(b_f.reshape(1, N_HEADS), ((0, 0), (0, LANES - N_HEADS)))
    qg = jnp.tile(q_g.reshape(1, HEAD_DIM), (1, 2))
    kg = jnp.tile(k_g.reshape(1, HEAD_DIM), (1, 2))
    row = lambda i: (i, 0)
    const = lambda i: (0, 0)
    big = jax.ShapeDtypeStruct((n, d), f32)
    return pl.pallas_call(
        _attn_proj_kernel,
        grid=(n // tm,),
        in_specs=[pl.BlockSpec((tm, d), row), pl.BlockSpec((1, d), const),
                  pl.BlockSpec((d, 4 * d), const), pl.BlockSpec((d, LANES), const),
                  pl.BlockSpec((1, LANES), const), pl.BlockSpec((1, LANES), const),
                  pl.BlockSpec((1, LANES), const)],
        out_specs=[pl.BlockSpec((tm, d), row)] * 4 + [pl.BlockSpec((tm, N_HEADS), row)],
        out_shape=[big, big, big, big, jax.ShapeDtypeStruct((n, N_HEADS), f32)],
        compiler_params=_cparams("parallel"),
        name="attn_proj",
    )(x2d, norm_g.reshape(1, d), w_main, wf, bfp, qg, kg)


def _out_proj_kernel(x_ref, o_ref, g_ref, w_ref, y_ref):
    g = g_ref[...]
    a = o_ref[...] * (g * jax.nn.sigmoid(g))
    y_ref[...] = x_ref[...] + _dot(a.astype(bf16), w_ref[...])


def out_project(x2d, o2d, g2d, w_out):
    n, d = x2d.shape
    tm = _row_tile(n)
    row = lambda i: (i, 0)
    return pl.pallas_call(
        _out_proj_kernel,
        grid=(n // tm,),
        in_specs=[pl.BlockSpec((tm, d), row)] * 3 + [pl.BlockSpec((d, d), lambda i: (0, 0))],
        out_specs=pl.BlockSpec((tm, d), row),
        out_shape=jax.ShapeDtypeStruct((n, d), f32),
        compiler_params=_cparams("parallel"),
        name="out_proj",
    )(x2d, o2d, g2d, w_out.astype(bf16))


ATT_TILE = 256
Q_TILE = 512
PEN_LANE0 = 16


def _lane_ids():
    lane = lax.broadcasted_iota(jnp.int32, (1, LANES), 1)
    return lane, lane & (HEAD_DIM - 1)


def _fox_extras(lf_ref, cp_ref, seq):
    hp = pl.program_id(1)
    lane, lmod = _lane_ids()
    hrow = lax.broadcasted_iota(jnp.int32, (N_HEADS, LANES), 0)
    hlane = lax.broadcasted_iota(jnp.int32, (N_HEADS, LANES), 1)
    place = (((hrow == 2 * hp) & (hlane >= HEAD_DIM) & (hlane < HEAD_DIM + 6))
             | ((hrow == 2 * hp + 1) & (hlane < 6)))
    place = jnp.where(place, 1.0, 0.0).astype(bf16)
    l1, l2, l3 = _split3(lf_ref[0])
    lfp = _dot(l1, place) + _dot(l2, place) + _dot(l3, place)
    r = lax.broadcasted_iota(jnp.int32, (ATT_TILE, ATT_TILE), 0)
    c = lax.broadcasted_iota(jnp.int32, (ATT_TILE, ATT_TILE), 1)
    tri = jnp.where(r >= c, 1.0, 0.0).astype(bf16)
    carry = jnp.zeros((1, LANES), f32)
    for n in range(seq // ATT_TILE):
        cb = _dot_exact_lhs(tri, lfp[n * ATT_TILE:(n + 1) * ATT_TILE]) + carry
        carry = cb[ATT_TILE - 1:ATT_TILE, :]
        cp_ref[n * ATT_TILE:(n + 1) * ATT_TILE, :] = cb
    cum = cp_ref[...]
    p1 = cum.astype(bf16).astype(f32)
    r1 = cum - p1
    p2 = r1.astype(bf16).astype(f32)
    p3 = (r1 - p2).astype(bf16).astype(f32)
    third = jnp.where(lmod >= 3, lmod - 3, lmod)
    parts = jnp.where(third == 0, p1, jnp.where(third == 1, p2, p3))
    qx = jnp.where(lmod < 3, parts, jnp.where(lmod < 6, 1.0, 0.0))
    kx = jnp.where(lmod < 3, 1.0, jnp.where(lmod < 6, -parts, 0.0))
    return qx, kx


def _moba_extras(sp_ref, seq):
    lane, lmod = _lane_ids()
    sp = sp_ref[0, 0:1, :]
    pos = lax.broadcasted_iota(jnp.int32, (seq, LANES), 0)
    hi = (pos >> 7).astype(f32)
    lo = (pos & 127).astype(f32)
    blk = pos >> 8
    qx = jnp.where(lmod < 6, sp, jnp.where(lmod < 9, -hi, jnp.where(lmod < 12, -lo, 0.0)))
    onehot = jnp.where((lmod >= PEN_LANE0) & (lmod < PEN_LANE0 + 8) & (blk == lmod - PEN_LANE0), 1.0, 0.0)
    kx = jnp.where(lmod < 3, 128.0 * hi,
                   jnp.where(lmod < 6, lo,
                             jnp.where(lmod < 9, 128.0 * sp, jnp.where(lmod < 12, sp, onehot))))
    return qx, kx


def _moba_penalty(gate, own, lane0):
    lane = lax.broadcasted_iota(jnp.int32, (1, LANES), 1)
    lane_f = lane.astype(f32)
    cand = (lane >= lane0) & (lane < lane0 + own)
    g = jnp.where(cand, gate, -3e38)
    sel = jnp.zeros(gate.shape, jnp.bool_)
    for _ in range(MOBA_TOPK):
        mx = jnp.max(g, axis=-1, keepdims=True)
        idx = jnp.min(jnp.where(g == mx, lane_f, 4.0 * LANES), axis=-1, keepdims=True)
        pick = (lane_f == idx) & (mx > -3e38)
        sel = sel | pick
        g = jnp.where(pick, -3e38, g)
    return jnp.where(cand & jnp.logical_not(sel), NEG_BIG, 0.0)


def _flash_kernel(q_ref, k_ref, v_ref, aux_ref, o_ref, qx_ref, ka0_ref, ka1_ref, vb_ref, cp_ref,
                  kmr_ref, *, mode, seq):
    first = _first_head_lanes()
    scale = HEAD_DIM ** -0.5
    if mode == "fox":
        qx, kx = _fox_extras(aux_ref, cp_ref, seq)
    else:
        qx, kx = _moba_extras(aux_ref, seq)
        nblk = seq // MOBA_BLOCK
        kmr_ref[...] = jnp.zeros((LANES, LANES), f32)
        for n in range(nblk):
            km = jnp.sum(k_ref[0, n * MOBA_BLOCK:(n + 1) * MOBA_BLOCK, :], axis=0, keepdims=True) * (1.0 / MOBA_BLOCK)
            r0 = HEAD_DIM + PEN_LANE0 + n
            kmr_ref[r0:r0 + 1, :] = jnp.where(first, km, 0.0)
            kmr_ref[PEN_LANE0 + n:PEN_LANE0 + n + 1, :] = jnp.where(first, 0.0, km)
    qx_ref[...] = qx
    kk = k_ref[0]
    ka0_ref[...] = jnp.where(first, kk, kx).astype(bf16)
    ka1_ref[...] = jnp.where(first, kx, kk).astype(bf16)
    vb_ref[...] = v_ref[0].astype(bf16)

    tq = min(seq, Q_TILE)
    rr = lax.broadcasted_iota(jnp.int32, (tq, tq), 0)
    cc = lax.broadcasted_iota(jnp.int32, (tq, tq), 1)
    causal = cc <= rr
    for qi in range(seq // tq):
        row0 = qi * tq
        q_raw = q_ref[0, row0:row0 + tq, :]
        qxt = qx_ref[row0:row0 + tq, :]
        if mode == "moba":
            gate = _dot_nt(q_raw, kmr_ref[...], precision=lax.Precision.HIGHEST)
            own = (row0 + lax.broadcasted_iota(jnp.int32, (tq, 1), 0)) // MOBA_BLOCK
        outs = []
        for hh in (0, 1):
            own_lanes = first if hh == 0 else jnp.logical_not(first)
            qa = jnp.where(own_lanes, q_raw * scale, qxt)
            if mode == "moba":
                qa = qa + _moba_penalty(gate, own, (HEAD_DIM if hh == 0 else 0) + PEN_LANE0)
            qa = qa.astype(bf16)
            ka_ref = ka0_ref if hh == 0 else ka1_ref
            s_diag = jnp.where(causal, _dot_nt(qa, ka_ref[row0:row0 + tq, :]), NEG_BIG)
            m = jnp.max(s_diag, axis=-1, keepdims=True)
            if row0:
                s_past = _dot_nt(qa, ka_ref[0:row0, :])
                m = jnp.maximum(m, jnp.max(s_past, axis=-1, keepdims=True))
            p = jnp.exp(s_diag - m)
            l = jnp.sum(p, axis=-1, keepdims=True)
            acc = _dot(p.astype(bf16), vb_ref[row0:row0 + tq, :])
            if row0:
                p = jnp.exp(s_past - m)
                l = l + jnp.sum(p, axis=-1, keepdims=True)
                acc = acc + _dot(p.astype(bf16), vb_ref[0:row0, :])
            outs.append(acc / l)
        o_ref[0, row0:row0 + tq, :] = jnp.where(first, outs[0], outs[1])


def _alibi_piece_table():
    slopes = jnp.asarray(2.0 ** (-8.0 * np.arange(1, N_HEADS + 1, dtype=np.float32) / N_HEADS), f32)
    p1, p2, p3 = (p.astype(f32) for p in _split3(slopes))
    pieces = jnp.stack([p1, p2, p3], axis=-1)
    lane = np.arange(LANES)
    lmod = lane % HEAD_DIM
    owner = np.where(lane < HEAD_DIM, 1, 0)
    tab = pieces[2 * np.arange(HEAD_PAIRS)[:, None] + owner[None, :], (lmod % 3)[None, :]]
    tab = jnp.where((lmod < 12)[None, :], tab, 0.0)
    return jnp.broadcast_to(tab[:, None, :], (HEAD_PAIRS, 8, LANES))


def prompt_attention(q, k, v, logf=None):
    nb, seq, d = q.shape
    assert seq % ATT_TILE == 0 and seq % min(seq, Q_TILE) == 0 and seq // MOBA_BLOCK <= 8 and seq <= 128 * 256
    mode = "moba" if logf is None else "fox"
    qkv_spec = pl.BlockSpec((1, seq, LANES), lambda b, hp: (b, 0, hp))
    if mode == "fox":
        aux, aux_spec = logf, pl.BlockSpec((1, seq, N_HEADS), lambda b, hp: (b, 0, 0))
    else:
        aux, aux_spec = _alibi_piece_table(), pl.BlockSpec((1, 8, LANES), lambda b, hp: (hp, 0, 0))
    return pl.pallas_call(
        functools.partial(_flash_kernel, mode=mode, seq=seq),
        grid=(nb, HEAD_PAIRS),
        in_specs=[qkv_spec, qkv_spec, qkv_spec, aux_spec],
        out_specs=qkv_spec,
        out_shape=jax.ShapeDtypeStruct((nb, seq, d), f32),
        scratch_shapes=[pltpu.VMEM((seq, LANES), f32), pltpu.VMEM((seq, LANES), bf16),
                        pltpu.VMEM((seq, LANES), bf16), pltpu.VMEM((seq, LANES), bf16),
                        pltpu.VMEM((seq, LANES), f32), pltpu.VMEM((LANES, LANES), f32)],
        compiler_params=_cparams("parallel", "parallel"),
        name=f"{mode}_prompt_attn",
    )(q, k, v, aux)


def _rms_kernel(x_ref, ng_ref, h_ref):
    x = x_ref[...]
    ms = jnp.mean(x * x, axis=-1, keepdims=True)
    h_ref[...] = x * lax.rsqrt(ms + RMS_EPS) * ng_ref[...]


def rms_norm_rows(x2d, norm_g):
    n, d = x2d.shape
    tm = _row_tile(n)
    row = lambda i: (i, 0)
    return pl.pallas_call(
        _rms_kernel, grid=(n // tm,),
        in_specs=[pl.BlockSpec((tm, d), row), pl.BlockSpec((1, d), lambda i: (0, 0))],
        out_specs=pl.BlockSpec((tm, d), row),
        out_shape=jax.ShapeDtypeStruct((n, d), f32),
        compiler_params=_cparams("parallel"), name="rms_norm",
    )(x2d, norm_g.reshape(1, d))


def _rwkv_proj_kernel(h_ref, hp_ref, mu_ref, w_ref, w0_ref, w1_ref, w2_ref, a0_ref, a1_ref, a2_ref,
                      kk_ref, ka_ref, rk_ref,
                      r_ref, dec_ref, k_ref, v_ref, na_ref, b_ref, g_ref, bonus_ref):
    d = D_MODEL
    h = h_ref[...]
    delta = hp_ref[...] - h

    def mixed(i):
        return (h + delta * mu_ref[i:i + 1, :]).astype(bf16)

    r = _dot(mixed(0), w_ref[:, 0:d])
    k = _dot(mixed(1), w_ref[:, d:2 * d])
    v = _dot(mixed(2), w_ref[:, 2 * d:3 * d])
    g_ref[...] = _dot(mixed(3), w_ref[:, 3 * d:4 * d])
    ww = w0_ref[...] + _dot(jnp.tanh(_dot(mixed(4), w1_ref[...])).astype(bf16), w2_ref[...])
    dec_ref[...] = jnp.exp(-jnp.exp(_log_sigmoid(ww) - 0.5))
    aa = jax.nn.sigmoid(a0_ref[...] + _dot(_dot(mixed(5), a1_ref[...]).astype(bf16), a2_ref[...]))
    k2 = k * (1.0 + (aa - 1.0) * ka_ref[...])
    kkv = k * kk_ref[...]
    rkk = r * k2 * rk_ref[...]
    r_ref[...] = r
    k_ref[...] = k2
    v_ref[...] = v
    for c in range(d // LANES):
        sl = slice(c * LANES, (c + 1) * LANES)
        kc = kkv[:, sl]
        kn = kc / jnp.maximum(jnp.sqrt(_per_head_sum(kc * kc)), 1e-12)
        na_ref[:, sl] = -kn
        b_ref[:, sl] = kn * aa[:, sl]
        bonus_ref[:, sl] = _per_head_sum(rkk[:, sl]) * v[:, sl]


def rwkv_project(h2d, hprev2d, mu, w_in, w0, w1, w2, a0, a1, a2, k_k, k_a, r_k):
    n, d = h2d.shape
    tm = min(_row_tile(n), 256)
    row = lambda i: (i, 0)
    const = lambda i: (0, 0)
    lw, la = w1.shape[1], a1.shape[1]
    vec = lambda z: z.reshape(1, d)
    big = jax.ShapeDtypeStruct((n, d), f32)
    return pl.pallas_call(
        _rwkv_proj_kernel, grid=(n // tm,),
        in_specs=[pl.BlockSpec((tm, d), row), pl.BlockSpec((tm, d), row), pl.BlockSpec((6, d), const),
                  pl.BlockSpec((d, 4 * d), const), pl.BlockSpec((1, d), const),
                  pl.BlockSpec((d, lw), const), pl.BlockSpec((lw, d), const), pl.BlockSpec((1, d), const),
                  pl.BlockSpec((d, la), const), pl.BlockSpec((la, d), const),
                  pl.BlockSpec((1, d), const), pl.BlockSpec((1, d), const), pl.BlockSpec((1, d), const)],
        out_specs=[pl.BlockSpec((tm, d), row)] * 8,
        out_shape=[big] * 8,
        compiler_params=_cparams("parallel"), name="rwkv_proj",
    )(h2d, hprev2d, mu, w_in.astype(bf16), vec(w0), w1.astype(bf16), w2.astype(bf16), vec(a0),
      a1.astype(bf16), a2.astype(bf16), vec(k_k), vec(k_a), vec(r_k))


SCAN_CHUNK = 32


def _rwkv_scan_kernel(r_ref, w_ref, k_ref, v_ref, na_ref, b_ref, s0_ref, y_ref, st_ref, *, tc):
    @pl.when(pl.program_id(1) == 0)
    def _():
        st_ref[...] = s0_ref[...]

    zeros = jnp.zeros((HEAD_DIM, LANES), f32)

    def step(t, carry):
        def sa_pass(j, sa):
            return sa + st_ref[j] * na_ref[t, pl.ds(j, 1), :]

        sa = lax.fori_loop(0, HEAD_DIM, sa_pass, zeros, unroll=16)
        vt = v_ref[t]

        def update_pass(j, y):
            s_new = (st_ref[j] * w_ref[t, pl.ds(j, 1), :] + sa * b_ref[t, pl.ds(j, 1), :]
                     + vt * k_ref[t, pl.ds(j, 1), :])
            st_ref[j] = s_new
            return y + s_new * r_ref[t, pl.ds(j, 1), :]

        y_ref[t] = lax.fori_loop(0, HEAD_DIM, update_pass, zeros, unroll=16)
        return carry

    lax.fori_loop(0, tc, step, 0)


def rwkv_scan(r, w, k, v, na, b, s0):
    nb, t, d = r.shape
    lanes = nb * N_HEADS
    assert lanes % LANES == 0
    tc = min(t, SCAN_CHUNK)
    assert t % tc == 0

    def to_scan(z):
        return z.reshape(nb, t, N_HEADS, HEAD_DIM).transpose(1, 3, 0, 2).reshape(t, HEAD_DIM, lanes)

    s0_l = s0.transpose(3, 2, 0, 1).reshape(HEAD_DIM, HEAD_DIM, lanes)
    seq_spec = pl.BlockSpec((tc, HEAD_DIM, LANES), lambda g, i: (i, 0, g))
    st_spec = pl.BlockSpec((HEAD_DIM, HEAD_DIM, LANES), lambda g, i: (0, 0, g))
    y, st = pl.pallas_call(
        functools.partial(_rwkv_scan_kernel, tc=tc),
        grid=(lanes // LANES, t // tc),
        in_specs=[seq_spec] * 6 + [st_spec],
        out_specs=[seq_spec, st_spec],
        out_shape=[jax.ShapeDtypeStruct((t, HEAD_DIM, lanes), f32),
                   jax.ShapeDtypeStruct((HEAD_DIM, HEAD_DIM, lanes), f32)],
        compiler_params=_cparams("parallel", "arbitrary"), name="rwkv_scan",
    )(*(to_scan(z) for z in (r, w, k, v, na, b)), s0_l)
    y = y.reshape(t, HEAD_DIM, nb, N_HEADS).transpose(2, 0, 3, 1).reshape(nb, t, d)
    st = st.reshape(HEAD_DIM, HEAD_DIM, nb, N_HEADS).transpose(2, 3, 1, 0)
    return y, st


def _rwkv_post_kernel(x_ref, y_ref, bonus_ref, g_ref, lng_ref, lnb_ref, w_ref, o_ref, a_ref):
    for c in range(D_MODEL // LANES):
        sl = slice(c * LANES, (c + 1) * LANES)
        y = y_ref[:, sl]
        dev = y - _per_head_sum(y) * (1.0 / HEAD_DIM)
        var = _per_head_sum(dev * dev) * (1.0 / HEAD_DIM)
        yn = dev * lax.rsqrt(var + GN_EPS) * lng_ref[:, sl] + lnb_ref[:, sl] + bonus_ref[:, sl]
        g = g_ref[:, sl]
        a_ref[:, sl] = (yn * (g * jax.nn.sigmoid(g))).astype(bf16)
    o_ref[...] = x_ref[...] + _dot(a_ref[...], w_ref[...])


def rwkv_post(x2d, y2d, bonus2d, g2d, ln_g, ln_b, w_out):
    n, d = x2d.shape
    tm = _row_tile(n)
    row = lambda i: (i, 0)
    const = lambda i: (0, 0)
    return pl.pallas_call(
        _rwkv_post_kernel, grid=(n // tm,),
        in_specs=[pl.BlockSpec((tm, d), row)] * 4 + [pl.BlockSpec((1, d), const)] * 2 + [pl.BlockSpec((d, d), const)],
        out_specs=pl.BlockSpec((tm, d), row),
        out_shape=jax.ShapeDtypeStruct((n, d), f32),
        scratch_shapes=[pltpu.VMEM((tm, d), bf16)],
        compiler_params=_cparams("parallel"), name="rwkv_post",
    )(x2d, y2d, bonus2d, g2d, ln_g.reshape(1, d), ln_b.reshape(1, d), w_out.astype(bf16))


def rwkv_layer(x, x_last, s0, norm_g, mu, w_in, w0, w1, w2, a0, a1, a2, k_k, k_a, r_k, ln_g, ln_b, w_out):
    nb, t, d = x.shape
    x2d = x.reshape(nb * t, d)
    hn = rms_norm_rows(x2d, norm_g).reshape(nb, t, d)
    hprev = jnp.concatenate([x_last[:, None, :], hn[:, :-1]], axis=1)
    r, dec, k, v, na, b, g, bonus = rwkv_project(hn.reshape(nb * t, d), hprev.reshape(nb * t, d), mu, w_in,
                                                 w0, w1, w2, a0, a1, a2, k_k, k_a, r_k.reshape(-1))
    shp = (nb, t, d)
    y, st = rwkv_scan(*(z.reshape(shp) for z in (r, dec, k, v, na, b)), s0)
    out = rwkv_post(x2d, y.reshape(nb * t, d), bonus, g, ln_g, ln_b, w_out)
    return out.reshape(nb, t, d), st, hn[:, -1]


PAGES_PER_STEP = 8


def _pages_as_matrices(pool):
    moved = jnp.moveaxis(pool, 2, -1)
    return moved.reshape(moved.shape[:2] + (-1, moved.shape[-1]))


def _page_specs(pool, layer, n_arrays=PAGES_PER_STEP):
    blk = (1, 1) + pool.shape[2:]
    zeros = (0,) * (pool.ndim - 2)
    return [pl.BlockSpec(blk, functools.partial(
        lambda b, jg, pt, p: (layer, pt[b, jg * PAGES_PER_STEP + p]) + zeros, p=p)) for p in range(n_arrays)]


def _block_diag_queries(qh):
    nb, _, t, _ = qh.shape
    eye = jnp.eye(N_HEADS, dtype=qh.dtype)
    return jnp.einsum("bhtd,hg->bhtgd", qh, eye).reshape(nb, N_HEADS * t, D_MODEL).astype(bf16)


def _page_scores(qbd_ref, k_ref):
    return _dot(qbd_ref[0], k_ref[0, 0].astype(bf16))


def _own_head_block(full, t):
    head = lax.broadcasted_iota(jnp.int32, (N_HEADS * t, 1), 0) // t
    pair = jnp.zeros((N_HEADS * t, LANES), f32)
    for c in range(HEAD_PAIRS):
        pair = pair + jnp.where((head >> 1) == c, full[:, c * LANES:(c + 1) * LANES], 0.0)
    return jnp.where((head & 1) == 0, pair[:, :HEAD_DIM], pair[:, HEAD_DIM:])


def _softmax_pages(scores, v_refs, t, m_old=None):
    mx = scores[0]
    for s in scores[1:]:
        mx = jnp.maximum(mx, s)
    m_new = jnp.broadcast_to(jnp.max(mx, axis=-1, keepdims=True), mx.shape)
    if m_old is not None:
        m_new = jnp.maximum(m_new, m_old)
    probs = [jnp.exp(s - m_new) for s in scores]
    psum = probs[0]
    for p in probs[1:]:
        psum = psum + p
    l_new = jnp.broadcast_to(jnp.sum(psum, axis=-1, keepdims=True), psum.shape)
    full = _dot_nt(probs[0].astype(bf16), v_refs[0][0, 0].astype(bf16))
    for p, v_ref in zip(probs[1:], v_refs[1:]):
        full = full + _dot_nt(p.astype(bf16), v_ref[0, 0].astype(bf16))
    return m_new, l_new, _own_head_block(full, t)


def _head_rows_selector(t):
    r = lax.broadcasted_iota(jnp.int32, (N_HEADS * t, N_HEADS), 0)
    c = lax.broadcasted_iota(jnp.int32, (N_HEADS * t, N_HEADS), 1)
    return jnp.where((r >= c * t) & (r < (c + 1) * t), 1.0, 0.0).astype(bf16)


def _fox_decode_kernel(pt_ref, q_ref, *refs):
    k_refs = refs[0:PAGES_PER_STEP]
    v_refs = refs[PAGES_PER_STEP:2 * PAGES_PER_STEP]
    lf_refs = refs[2 * PAGES_PER_STEP:3 * PAGES_PER_STEP]
    m_ref, l_ref, acc_ref, tot_ref = refs[3 * PAGES_PER_STEP:]
    t = q_ref.shape[1] // N_HEADS

    @pl.when(pl.program_id(1) == 0)
    def _():
        m_ref[...] = jnp.full(m_ref.shape, NEG_BIG, f32)
        l_ref[...] = jnp.zeros(l_ref.shape, f32)
        acc_ref[...] = jnp.zeros(acc_ref.shape, f32)
        tot_ref[...] = jnp.zeros(tot_ref.shape, f32)

    scale = HEAD_DIM ** -0.5
    r = lax.broadcasted_iota(jnp.int32, (PAGE_SIZE, PAGE_SIZE), 0)
    c = lax.broadcasted_iota(jnp.int32, (PAGE_SIZE, PAGE_SIZE), 1)
    upper = jnp.where(r <= c, 1.0, 0.0).astype(bf16)
    repeat_rows = jnp.concatenate([_head_rows_selector(t)] * 3, axis=1)
    tot = tot_ref[0]
    scores = []
    for p in range(PAGES_PER_STEP):
        pieces = jnp.concatenate(_split3(lf_refs[p][0, 0]), axis=0)
        local3 = _dot(pieces, upper)
        local = local3[0:N_HEADS] + local3[N_HEADS:2 * N_HEADS] + local3[2 * N_HEADS:]
        cum = _dot(repeat_rows, jnp.concatenate(_split3(tot + local), axis=0))
        tot = tot + jnp.broadcast_to(local[:, PAGE_SIZE - 1:PAGE_SIZE], local.shape)
        scores.append(_page_scores(q_ref, k_refs[p]) * scale - cum)
    tot_ref[0] = tot
    m_old = m_ref[0]
    m_new, l_new, acc = _softmax_pages(scores, v_refs, t, m_old)
    alpha = jnp.exp(m_old - m_new)
    m_ref[0] = m_new
    l_ref[0] = alpha * l_ref[0] + l_new
    acc_ref[0] = alpha[:, :HEAD_DIM] * acc_ref[0] + acc


def _fox_finish_kernel(q_ref, k_ref, v_ref, lf_ref, lft_ref, m_ref, l_ref, acc_ref, tot_ref, o_ref):
    t = q_ref.shape[2]
    scale = HEAD_DIM ** -0.5
    r = lax.broadcasted_iota(jnp.int32, (t, t), 0)
    c = lax.broadcasted_iota(jnp.int32, (t, t), 1)
    lower = jnp.where(r >= c, 1.0, 0.0).astype(bf16)
    upper = jnp.where(r <= c, 1.0, 0.0).astype(bf16)
    cn_col = _dot_exact_lhs(lower, lf_ref[0])
    l1, l2, l3 = _split3(lft_ref[0])
    cn_row = _dot(l1, upper) + _dot(l2, upper) + _dot(l3, upper)
    for h in range(N_HEADS):
        rows = slice(h * t, (h + 1) * t)
        cq = cn_col[:, h:h + 1]
        s = _dot_nt(q_ref[0, h], k_ref[0, h]) * scale + cq - cn_row[h:h + 1, :]
        s = jnp.where(c <= r, s, NEG_BIG)
        m_past = m_ref[0, rows, 0:1] + cq + tot_ref[0, h:h + 1, 0:1]
        m_tot = jnp.maximum(m_past, jnp.max(s, axis=-1, keepdims=True))
        w_past = jnp.exp(m_past - m_tot)
        p = jnp.exp(s - m_tot)
        l_tot = w_past * l_ref[0, rows, 0:1] + jnp.sum(p, axis=-1, keepdims=True)
        o_ref[0, h] = (w_past * acc_ref[0, rows, :] + _dot(p, v_ref[0, h])) / l_tot


def _heads_first(z2d, nb, t):
    return z2d.reshape(nb, t, N_HEADS, HEAD_DIM).transpose(0, 2, 1, 3)


def _heads_last(z4d):
    nb, _, t, _ = z4d.shape
    return z4d.transpose(0, 2, 1, 3).reshape(nb * t, D_MODEL)


def fox_decode_attention(q2d, k2d, v2d, logf2d, pool_k, pool_v, pool_lf, layer, page_table):
    nb, n_pages = page_table.shape
    t = q2d.shape[0] // nb
    assert n_pages % PAGES_PER_STEP == 0
    qh, kh, vh = (_heads_first(z, nb, t) for z in (q2d, k2d, v2d))
    lf = logf2d.reshape(nb, t, N_HEADS)
    pool_k, pool_v = _pages_as_matrices(pool_k), _pages_as_matrices(pool_v)
    pool_lft = jnp.swapaxes(pool_lf, 2, 3)
    rows = N_HEADS * t
    head_spec = pl.BlockSpec((1, rows, D_MODEL), lambda b, jg, pt: (b, 0, 0))
    stat_spec = pl.BlockSpec((1, rows, PAGE_SIZE), lambda b, jg, pt: (b, 0, 0))
    acc_spec = pl.BlockSpec((1, rows, HEAD_DIM), lambda b, jg, pt: (b, 0, 0))
    tot_spec = pl.BlockSpec((1, N_HEADS, PAGE_SIZE), lambda b, jg, pt: (b, 0, 0))
    stat = jax.ShapeDtypeStruct((nb, rows, PAGE_SIZE), f32)
    m, l, acc, tot = pl.pallas_call(
        _fox_decode_kernel,
        grid_spec=pltpu.PrefetchScalarGridSpec(
            num_scalar_prefetch=1, grid=(nb, n_pages // PAGES_PER_STEP),
            in_specs=[head_spec] + _page_specs(pool_k, layer) + _page_specs(pool_v, layer) + _page_specs(pool_lft, layer),
            out_specs=[stat_spec, stat_spec, acc_spec, tot_spec]),
        out_shape=[stat, stat, jax.ShapeDtypeStruct((nb, rows, HEAD_DIM), f32),
                   jax.ShapeDtypeStruct((nb, N_HEADS, PAGE_SIZE), f32)],
        compiler_params=_cparams("parallel", "arbitrary"), name="fox_decode",
    )(page_table, _block_diag_queries(qh), *([pool_k] * PAGES_PER_STEP), *([pool_v] * PAGES_PER_STEP), *([pool_lft] * PAGES_PER_STEP))
    hs = pl.BlockSpec((1, N_HEADS, t, HEAD_DIM), lambda b: (b, 0, 0, 0))
    ss = pl.BlockSpec((1, rows, PAGE_SIZE), lambda b: (b, 0, 0))
    o = pl.pallas_call(
        _fox_finish_kernel, grid=(nb,),
        in_specs=[hs, hs, hs, pl.BlockSpec((1, t, N_HEADS), lambda b: (b, 0, 0)),
                  pl.BlockSpec((1, N_HEADS, t), lambda b: (b, 0, 0)), ss, ss,
                  pl.BlockSpec((1, rows, HEAD_DIM), lambda b: (b, 0, 0)),
                  pl.BlockSpec((1, N_HEADS, PAGE_SIZE), lambda b: (b, 0, 0))],
        out_specs=hs, out_shape=jax.ShapeDtypeStruct((nb, N_HEADS, t, HEAD_DIM), f32),
        compiler_params=_cparams("parallel"), name="fox_decode_finish",
    )(qh, kh, vh, lf, jnp.swapaxes(lf, 1, 2), m, l, acc, tot)
    return _heads_last(o)


def _alibi_slopes_np():
    return 2.0 ** (-8.0 * np.arange(1, N_HEADS + 1, dtype=np.float32) / N_HEADS)


def _moba_decode_kernel(pt_ref, q_ref, qt_ref, *refs, past_len):
    k_refs = refs[0:PAGES_PER_STEP]
    v_refs = refs[PAGES_PER_STEP:2 * PAGES_PER_STEP]
    slope_ref, m_ref, l_ref, acc_ref, gate_ref = refs[2 * PAGES_PER_STEP:]
    rows = q_ref.shape[1]
    t = rows // N_HEADS
    scale = HEAD_DIM ** -0.5
    pages_per_block = MOBA_BLOCK // PAGE_SIZE
    page0 = pl.program_id(1) * PAGES_PER_STEP
    qpos = past_len + lax.broadcasted_iota(jnp.int32, (rows, PAGE_SIZE), 0) % t
    lane = lax.broadcasted_iota(jnp.int32, (rows, PAGE_SIZE), 1)
    for blk in range(PAGES_PER_STEP // pages_per_block):
        pages = range(blk * pages_per_block, (blk + 1) * pages_per_block)
        kblock = k_refs[pages[0]][0, 0]
        for p in pages[1:]:
            kblock = kblock + k_refs[p][0, 0]
        kmean = jnp.sum(kblock, axis=-1, keepdims=True) * (1.0 / MOBA_BLOCK)
        for h in range(N_HEADS):
            kmean_h = kmean[h * HEAD_DIM:(h + 1) * HEAD_DIM, :]
            gate_ref[0, blk, h:h + 1, :] = jnp.sum(qt_ref[0, h] * kmean_h, axis=0, keepdims=True)
        scores = []
        for p in pages:
            dist = (qpos - ((page0 + p) * PAGE_SIZE + lane)).astype(f32)
            scores.append(_page_scores(q_ref, k_refs[p]) * scale - slope_ref[...] * dist)
        m, l, acc = _softmax_pages(scores, [v_refs[p] for p in pages], t)
        m_ref[0, blk] = m[:, :HEAD_DIM]
        l_ref[0, blk] = l[:, :HEAD_DIM]
        acc_ref[0, blk] = acc


def _moba_finish_kernel(q_ref, k_ref, v_ref, m_ref, l_ref, acc_ref, gate_ref, o_ref):
    t = q_ref.shape[2]
    n_blocks = m_ref.shape[1]
    scale = HEAD_DIM ** -0.5
    slopes = _alibi_slopes_np()
    r = lax.broadcasted_iota(jnp.int32, (t, t), 0)
    c = lax.broadcasted_iota(jnp.int32, (t, t), 1)
    lane = lax.broadcasted_iota(jnp.int32, (1, n_blocks), 1).astype(f32)
    g = gate_ref[0]
    sel = jnp.zeros(g.shape, f32)
    for _ in range(min(MOBA_TOPK, n_blocks)):
        mx = jnp.max(g, axis=-1, keepdims=True)
        idx = jnp.min(jnp.where(g == mx, lane, 4.0 * n_blocks), axis=-1, keepdims=True)
        pick = lane == idx
        sel = jnp.where(pick, 1.0, sel)
        g = jnp.where(pick, -3e38, g)
    for h in range(N_HEADS):
        qh = q_ref[0, h]
        s = _dot_nt(qh, k_ref[0, h]) * scale - float(slopes[h]) * (r - c).astype(f32)
        s = jnp.where(c <= r, s, NEG_BIG)
        m_own = jnp.max(s, axis=-1, keepdims=True)
        p = jnp.exp(s - m_own)
        l_own = jnp.sum(p, axis=-1, keepdims=True)
        acc_own = _dot(p, v_ref[0, h])
        rows = slice(h * t, (h + 1) * t)
        sel_h = sel[rows, :]
        chosen = [jnp.broadcast_to(sel_h[:, n:n + 1], (t, HEAD_DIM)) for n in range(n_blocks)]
        m_blk = [jnp.where(chosen[n] > 0.0, m_ref[0, n, rows, :], NEG_BIG) for n in range(n_blocks)]
        m_tot = jnp.broadcast_to(m_own, (t, HEAD_DIM))
        for n in range(n_blocks):
            m_tot = jnp.maximum(m_tot, m_blk[n])
        w_own = jnp.exp(m_own - m_tot)
        l_tot = w_own * l_own
        acc_tot = w_own * acc_own
        for n in range(n_blocks):
            w = jnp.exp(m_blk[n] - m_tot) * chosen[n]
            l_tot = l_tot + w * l_ref[0, n, rows, :]
            acc_tot = acc_tot + w * acc_ref[0, n, rows, :]
        o_ref[0, h] = acc_tot / l_tot


def moba_decode_attention(q2d, k2d, v2d, pool_k, pool_v, layer, page_table):
    nb, n_pages = page_table.shape
    t = q2d.shape[0] // nb
    past_len = n_pages * PAGE_SIZE
    assert past_len % MOBA_BLOCK == 0 and t <= MOBA_BLOCK and n_pages % PAGES_PER_STEP == 0
    n_blocks = past_len // MOBA_BLOCK
    bps = PAGES_PER_STEP * PAGE_SIZE // MOBA_BLOCK
    qh, kh, vh = (_heads_first(z, nb, t) for z in (q2d, k2d, v2d))
    qt = jnp.swapaxes(qh, 2, 3)
    pool_k, pool_v = _pages_as_matrices(pool_k), _pages_as_matrices(pool_v)
    rows = N_HEADS * t
    head_spec = pl.BlockSpec((1, rows, D_MODEL), lambda b, jg, pt: (b, 0, 0))
    qt_spec = pl.BlockSpec((1, N_HEADS, HEAD_DIM, t), lambda b, jg, pt: (b, 0, 0, 0))
    slope_rows = jnp.broadcast_to(jnp.repeat(jnp.asarray(_alibi_slopes_np()), t)[:, None], (rows, PAGE_SIZE))
    part_spec = pl.BlockSpec((1, bps, rows, HEAD_DIM), lambda b, jg, pt: (b, jg, 0, 0))
    part = jax.ShapeDtypeStruct((nb, n_blocks, rows, HEAD_DIM), f32)
    m, l, acc, gate = pl.pallas_call(
        functools.partial(_moba_decode_kernel, past_len=past_len),
        grid_spec=pltpu.PrefetchScalarGridSpec(
            num_scalar_prefetch=1, grid=(nb, n_pages // PAGES_PER_STEP),
            in_specs=([head_spec, qt_spec] + _page_specs(pool_k, layer) + _page_specs(pool_v, layer)
                      + [pl.BlockSpec((rows, PAGE_SIZE), lambda b, jg, pt: (0, 0))]),
            out_specs=[part_spec, part_spec, part_spec,
                       pl.BlockSpec((1, bps, N_HEADS, t), lambda b, jg, pt: (b, jg, 0, 0))]),
        out_shape=[part, part, part, jax.ShapeDtypeStruct((nb, n_blocks, N_HEADS, t), f32)],
        compiler_params=_cparams("parallel", "parallel"), name="moba_decode",
    )(page_table, _block_diag_queries(qh), qt, *([pool_k] * PAGES_PER_STEP), *([pool_v] * PAGES_PER_STEP), slope_rows)
    gate = gate.transpose(0, 2, 3, 1).reshape(nb, rows, n_blocks)
    hs = pl.BlockSpec((1, N_HEADS, t, HEAD_DIM), lambda b: (b, 0, 0, 0))
    ps = pl.BlockSpec((1, n_blocks, rows, HEAD_DIM), lambda b: (b, 0, 0, 0))
    o = pl.pallas_call(
        _moba_finish_kernel, grid=(nb,),
        in_specs=[hs, hs, hs, ps, ps, ps, pl.BlockSpec((1, N_HEADS * t, n_blocks), lambda b: (b, 0, 0))],
        out_specs=hs, out_shape=jax.ShapeDtypeStruct((nb, N_HEADS, t, HEAD_DIM), f32),
        compiler_params=_cparams("parallel"), name="moba_decode_finish",
    )(qh, kh, vh, m, l, acc, gate)
    return _heads_last(o)


N_MIXERS = 3


def kernel(x_prompt, x_sample, cache_fox_k, cache_fox_v, cache_fox_logf, state_rwkv_wkv, state_rwkv_shift, cache_moba_k, cache_moba_v, page_table, norm_g, fox_w_in, fox_b_f, fox_q_g, fox_k_g, fox_w_out, rwkv_mu, rwkv_w_in, rwkv_w0, rwkv_w1, rwkv_w2, rwkv_a0, rwkv_a1, rwkv_a2, rwkv_k_k, rwkv_k_a, rwkv_r_k, rwkv_ln_g, rwkv_ln_b, rwkv_w_out, moba_w_in, moba_q_g, moba_k_g, moba_w_out):
    nb, seq, d = x_prompt.shape
    db, dseq, _ = x_sample.shape
    depth = norm_g.shape[0]
    xp = x_prompt.reshape(nb * seq, d)
    xs = x_sample.reshape(db * dseq, d)
    heads_p = (nb, seq, N_HEADS, HEAD_DIM)
    heads_s = (db, dseq, N_HEADS, HEAD_DIM)
    outs = {name: [] for name in ("fk_p", "fv_p", "flf_p", "fk_s", "fv_s", "flf_s", "rw_p", "rsh_p", "rw_s", "rsh_s",
                                  "mk_p", "mv_p", "mk_s", "mv_s")}
    for i in range(depth):
        kind, j = i % N_MIXERS, i // N_MIXERS
        if kind == 0:
            args = (norm_g[i], fox_w_in[j], fox_q_g[j], fox_k_g[j], fox_b_f[j])
            qp, kp, vp, gp, lfp = attn_project(xp, *args)
            qs, ks, vs, gs, lfs = attn_project(xs, *args)
            op = prompt_attention(qp.reshape(nb, seq, d), kp.reshape(nb, seq, d), vp.reshape(nb, seq, d),
                                  lfp.reshape(nb, seq, N_HEADS)).reshape(nb * seq, d)
            os_ = fox_decode_attention(qs, ks, vs, lfs, cache_fox_k, cache_fox_v, cache_fox_logf, j, page_table)
            xp = out_project(xp, op, gp, fox_w_out[j])
            xs = out_project(xs, os_, gs, fox_w_out[j])
            outs["fk_p"].append(kp.reshape(heads_p)); outs["fv_p"].append(vp.reshape(heads_p))
            outs["flf_p"].append(lfp.reshape(nb, seq, N_HEADS))
            outs["fk_s"].append(ks.reshape(heads_s)); outs["fv_s"].append(vs.reshape(heads_s))
            outs["flf_s"].append(lfs.reshape(db, dseq, N_HEADS))
        elif kind == 1:
            wts = (norm_g[i], rwkv_mu[j], rwkv_w_in[j], rwkv_w0[j], rwkv_w1[j], rwkv_w2[j], rwkv_a0[j], rwkv_a1[j],
                   rwkv_a2[j], rwkv_k_k[j], rwkv_k_a[j], rwkv_r_k[j], rwkv_ln_g[j], rwkv_ln_b[j], rwkv_w_out[j])
            xp3, st_p, sh_p = rwkv_layer(xp.reshape(nb, seq, d), jnp.zeros((nb, d), f32),
                                         jnp.zeros((nb, N_HEADS, HEAD_DIM, HEAD_DIM), f32), *wts)
            xs3, st_s, sh_s = rwkv_layer(xs.reshape(db, dseq, d), state_rwkv_shift[j], state_rwkv_wkv[j], *wts)
            xp, xs = xp3.reshape(nb * seq, d), xs3.reshape(db * dseq, d)
            outs["rw_p"].append(st_p); outs["rsh_p"].append(sh_p)
            outs["rw_s"].append(st_s); outs["rsh_s"].append(sh_s)
        else:
            args = (norm_g[i], moba_w_in[j], moba_q_g[j], moba_k_g[j])
            qp, kp, vp, gp, _ = attn_project(xp, *args)
            qs, ks, vs, gs, _ = attn_project(xs, *args)
            op = prompt_attention(qp.reshape(nb, seq, d), kp.reshape(nb, seq, d),
                                  vp.reshape(nb, seq, d)).reshape(nb * seq, d)
            os_ = moba_decode_attention(qs, ks, vs, cache_moba_k, cache_moba_v, j, page_table)
            xp = out_project(xp, op, gp, moba_w_out[j])
            xs = out_project(xs, os_, gs, moba_w_out[j])
            outs["mk_p"].append(kp.reshape(heads_p)); outs["mv_p"].append(vp.reshape(heads_p))
            outs["mk_s"].append(ks.reshape(heads_s)); outs["mv_s"].append(vs.reshape(heads_s))
    stacked = {name: jnp.stack(v) for name, v in outs.items()}
    return (xp.reshape(nb, seq, d), xs.reshape(db, dseq, d),
            stacked["fk_p"], stacked["fv_p"], stacked["flf_p"], stacked["fk_s"], stacked["fv_s"], stacked["flf_s"],
            stacked["rw_p"], stacked["rsh_p"], stacked["rw_s"], stacked["rsh_s"],
            stacked["mk_p"], stacked["mv_p"], stacked["mk_s"], stacked["mv_s"])
```
